```python
import jax, jax.numpy as jnp
from jax import lax
import numpy as np

D_MODEL = 1024
BATCH = 16
SEQ = 2048
DEPTH = 4

N_EVEN = (DEPTH + 1) // 2
N_ODD = DEPTH // 2
A_HEADS = 4
A_HEAD_DIM = 128
A_WIDTH = A_HEADS * A_HEAD_DIM
A_CHUNK = 128
B_GROUPS = 4
B_GROUP_DIM = 128
B_WIDTH = B_GROUPS * B_GROUP_DIM
B_WINDOWS = (2, 4, 8, 16)
EVEN_IN = 2 * A_WIDTH + B_WIDTH
EVEN_OUT = A_WIDTH + B_WIDTH
C_HEADS = 4
C_QK_DIM = 128
C_V_DIM = 256
C_QK_WIDTH = C_HEADS * C_QK_DIM
C_V_WIDTH = C_HEADS * C_V_DIM
C_CHUNK = 128
C_CONV = 4
ODD_IN = 2 * C_QK_WIDTH + 2 * C_V_WIDTH + 2 * C_HEADS
D_FF = ((8 * D_MODEL // 3 + 255) // 256) * 256
PLE_DIM = 256
EPS = 1e-6

kernel_name = "hybrid_gmlp_pool_mlstm_trunk"


def rmsnorm(x, g):
    xf = x.astype(jnp.float32)
    y = xf * lax.rsqrt(jnp.mean(xf * xf, axis=-1, keepdims=True) + EPS)
    return (y * g.astype(jnp.float32)).astype(x.dtype)


def gmlp_chunk_mixer(u, v, v_gain, w_s, b_s):
    bsz, s, _ = u.shape
    u = jax.nn.gelu(u)
    v = rmsnorm(jax.nn.gelu(v), v_gain)
    nc = s // A_CHUNK
    vc = v.reshape(bsz, nc, A_CHUNK, A_HEADS, A_HEAD_DIM)
    mask = jnp.tril(jnp.ones((A_CHUNK, A_CHUNK), dtype=bool))
    w = jnp.where(mask, w_s, 0).astype(v.dtype)
    sv = jnp.einsum('hts,bcshd->bcthd', w, vc) + b_s.T.astype(v.dtype)[:, :, None]
    return u * sv.reshape(bsz, s, A_WIDTH)


def multiscale_pool_mixer(xb, w_pool, pool_scale):
    bsz, s, _ = xb.shape
    xg = xb.astype(jnp.float32).reshape(bsz, s, B_GROUPS, B_GROUP_DIM)
    cs = jnp.cumsum(xg, axis=1)
    t = jnp.arange(s)
    outs = []
    for g, win in enumerate(B_WINDOWS):
        csg = cs[:, :, g]
        lag = jnp.pad(csg, ((0, 0), (win, 0), (0, 0)))[:, :s]
        cnt = jnp.minimum(t + 1, win).astype(jnp.float32)[None, :, None]
        outs.append((csg - lag) / cnt - xg[:, :, g])
    pooled = jnp.stack(outs, axis=2).astype(xb.dtype)
    y = jnp.einsum('bsgd,gde->bsge', pooled, w_pool).reshape(bsz, s, B_WIDTH)
    return y * pool_scale


def even_mixer(xn, w_in, a_v_gain, a_ws, a_bs, b_wpool, b_scale, w_out):
    z = xn @ w_in
    u, v, xb = jnp.split(z, [A_WIDTH, 2 * A_WIDTH], axis=-1)
    ya = gmlp_chunk_mixer(u, v, a_v_gain, a_ws, a_bs)
    yb = multiscale_pool_mixer(xb, b_wpool, b_scale)
    return jnp.concatenate([ya, yb], axis=-1) @ w_out


def causal_short_conv(x, w):
    s = x.shape[1]
    y = x * w[0]
    for k in range(1, C_CONV):
        y = y + jnp.pad(x, ((0, 0), (k, 0), (0, 0)))[:, :s] * w[k]
    return y


def mlstm_chunkwise(q, k, v, ig, lf):
    bsz, s = q.shape[:2]
    nc = s // C_CHUNK
    L = C_CHUNK

    def to_chunks(a):
        a = a.reshape((bsz, nc, L) + a.shape[2:])
        a = jnp.moveaxis(a, 1, 0)
        return jnp.swapaxes(a, 2, 3)

    qc, kc, vc, ic, fc = (to_chunks(a) for a in (q, k, v, ig, lf))
    mask = jnp.tril(jnp.ones((L, L), dtype=bool))

    def step(carry, inp):
        c_st, n_st, m_st = carry
        qb, kb, vb, ib, fb = inp
        b = jnp.cumsum(fb, axis=-1)
        dlog = jnp.where(mask, b[..., :, None] - b[..., None, :] + ib[..., None, :], -jnp.inf)
        a = b + m_st[..., None]
        mt = jnp.maximum(a, jnp.max(dlog, axis=-1))
        wi = jnp.exp(a - mt)
        sc = jnp.einsum('bhtd,bhsd->bhts', qb, kb) * jnp.exp(dlog - mt[..., None])
        num = wi[..., None] * jnp.einsum('bhtd,bhde->bhte', qb, c_st) + jnp.einsum('bhts,bhse->bhte', sc, vb)
        den = wi * jnp.einsum('bhtd,bhd->bht', qb, n_st) + jnp.sum(sc, axis=-1)
        h = num / jnp.maximum(jnp.abs(den), jnp.exp(-mt))[..., None]
        b_last = b[..., -1]
        g = b_last[..., None] - b + ib
        m_new = jnp.maximum(b_last + m_st, jnp.max(g, axis=-1))
        wc = jnp.exp(b_last + m_st - m_new)
        ws = jnp.exp(g - m_new[..., None])
        c_new = wc[..., None, None] * c_st + jnp.einsum('bhs,bhsd,bhse->bhde', ws, kb, vb)
        n_new = wc[..., None] * n_st + jnp.einsum('bhs,bhsd->bhd', ws, kb)
        return (c_new, n_new, m_new), h

    init = (jnp.zeros((bsz, C_HEADS, C_QK_DIM, C_V_DIM), jnp.float32),
            jnp.zeros((bsz, C_HEADS, C_QK_DIM), jnp.float32),
            jnp.zeros((bsz, C_HEADS), jnp.float32))
    _, hs = lax.scan(step, init, (qc, kc, vc, ic, fc))
    hs = jnp.moveaxis(jnp.swapaxes(hs, 2, 3), 0, 1)
    return hs.reshape(bsz, s, C_HEADS, C_V_DIM)


def odd_mixer(xn, w_in, conv_w, b_i, b_f, h_gain, w_out):
    bsz, s, _ = xn.shape
    z = xn @ w_in
    o1 = 2 * C_QK_WIDTH
    o2 = o1 + C_V_WIDTH
    o3 = o2 + C_V_WIDTH
    o4 = o3 + C_HEADS
    qk, v, o, gi, gf = jnp.split(z, [o1, o2, o3, o4], axis=-1)
    qk = jax.nn.silu(causal_short_conv(qk, conv_w))
    q, k = jnp.split(qk, 2, axis=-1)
    q = q.reshape(bsz, s, C_HEADS, C_QK_DIM).astype(jnp.float32) * (C_QK_DIM ** -0.5)
    k = k.reshape(bsz, s, C_HEADS, C_QK_DIM).astype(jnp.float32)
    v = v.reshape(bsz, s, C_HEADS, C_V_DIM).astype(jnp.float32)
    ig = (gi + b_i).astype(jnp.float32)
    lf = jax.nn.log_sigmoid((gf + b_f).astype(jnp.float32))
    h = mlstm_chunkwise(q, k, v, ig, lf)
    h = rmsnorm(h, h_gain.reshape(C_HEADS, C_V_DIM)).astype(xn.dtype).reshape(bsz, s, C_V_WIDTH)
    return (h * jax.nn.sigmoid(o)) @ w_out


def swiglu(x, w_gate_up, w_down):
    g, u = jnp.split(x @ w_gate_up, 2, axis=-1)
    return (jax.nn.silu(g) * u) @ w_down


def setup_inputs(seed: int = 0) -> dict:
    key = jax.random.key(seed)
    ks = iter(jax.random.split(key, 32))
    nrm = lambda shape, scale: jax.random.normal(next(ks), shape, jnp.float32) * scale
    gain = lambda shape: 1.0 + 0.1 * jax.random.normal(next(ks), shape, jnp.float32)
    b_f = (jnp.linspace(3.0, 6.0, C_HEADS, dtype=jnp.float32)[None, :]
           + 0.1 * jax.random.normal(next(ks), (N_ODD, C_HEADS), jnp.float32))
    return {
        "x": nrm((BATCH, SEQ, D_MODEL), 1.0),
        "p": nrm((DEPTH, BATCH, SEQ, PLE_DIM), 1.0),
        "mix_pre_gain": gain((DEPTH, D_MODEL)),
        "mix_post_gain": gain((DEPTH, D_MODEL)),
        "ffn_pre_gain": gain((DEPTH, D_MODEL)),
        "ffn_post_gain": gain((DEPTH, D_MODEL)),
        "ple_post_gain": gain((DEPTH, D_MODEL)),
        "even_w_in": nrm((N_EVEN, D_MODEL, EVEN_IN), D_MODEL ** -0.5),
        "even_a_v_gain": gain((N_EVEN, A_WIDTH)),
        "even_a_ws": nrm((N_EVEN, A_HEADS, A_CHUNK, A_CHUNK), 0.5 * A_CHUNK ** -0.5),
        "even_a_bs": gain((N_EVEN, A_HEADS, A_CHUNK)),
        "even_b_wpool": nrm((N_EVEN, B_GROUPS, B_GROUP_DIM, B_GROUP_DIM), B_GROUP_DIM ** -0.5),
        "even_b_scale": gain((N_EVEN, B_WIDTH)),
        "even_w_out": nrm((N_EVEN, EVEN_OUT, D_MODEL), EVEN_OUT ** -0.5),
        "odd_w_in": nrm((N_ODD, D_MODEL, ODD_IN), D_MODEL ** -0.5),
        "odd_conv_w": nrm((N_ODD, C_CONV, 2 * C_QK_WIDTH), C_CONV ** -0.5),
        "odd_b_i": nrm((N_ODD, C_HEADS), 0.1),
        "odd_b_f": b_f,
        "odd_h_gain": gain((N_ODD, C_V_WIDTH)),
        "odd_w_out": nrm((N_ODD, C_V_WIDTH, D_MODEL), C_V_WIDTH ** -0.5),
        "ffn_w_gate_up": nrm((DEPTH, D_MODEL, 2 * D_FF), D_MODEL ** -0.5),
        "ffn_w_down": nrm((DEPTH, D_FF, D_MODEL), D_FF ** -0.5),
        "ple_proj": nrm((DEPTH, PLE_DIM, D_MODEL), PLE_DIM ** -0.5),
        "ple_gate": nrm((DEPTH, D_MODEL, D_MODEL), D_MODEL ** -0.5),
    }


def reference(x, p, mix_pre_gain, mix_post_gain, ffn_pre_gain, ffn_post_gain, ple_post_gain,
              even_w_in, even_a_v_gain, even_a_ws, even_a_bs, even_b_wpool, even_b_scale, even_w_out,
              odd_w_in, odd_conv_w, odd_b_i, odd_b_f, odd_h_gain, odd_w_out,
              ffn_w_gate_up, ffn_w_down, ple_proj, ple_gate):
    for i in range(DEPTH):
        j = i // 2
        h = rmsnorm(x, mix_pre_gain[i])
        if i % 2 == 0:
            y = even_mixer(h, even_w_in[j], even_a_v_gain[j], even_a_ws[j], even_a_bs[j],
                           even_b_wpool[j], even_b_scale[j], even_w_out[j])
        else:
            y = odd_mixer(h, odd_w_in[j], odd_conv_w[j], odd_b_i[j], odd_b_f[j],
                          odd_h_gain[j], odd_w_out[j])
        x = x + rmsnorm(y, mix_post_gain[i])
        y = swiglu(rmsnorm(x, ffn_pre_gain[i]), ffn_w_gate_up[i], ffn_w_down[i])
        x = x + rmsnorm(y, ffn_post_gain[i])
        e = jax.nn.sigmoid(x @ ple_gate[i]) * (p[i] @ ple_proj[i])
        x = x + rmsnorm(e, ple_post_gain[i])
    return x
```

```python
import functools

import jax
import jax.numpy as jnp
from jax import lax
from jax.experimental import pallas as pl
from jax.experimental.pallas import tpu as pltpu

D_MODEL = 1024
A_HEADS = 4
A_HEAD_DIM = 128
A_WIDTH = A_HEADS * A_HEAD_DIM
A_CHUNK = 128
B_GROUPS = 4
B_GROUP_DIM = 128
B_WIDTH = B_GROUPS * B_GROUP_DIM
B_WINDOWS = (2, 4, 8, 16)
EVEN_IN = 2 * A_WIDTH + B_WIDTH
EVEN_OUT = A_WIDTH + B_WIDTH
C_HEADS = 4
C_QK_DIM = 128
C_V_DIM = 256
C_QK_WIDTH = C_HEADS * C_QK_DIM
C_V_WIDTH = C_HEADS * C_V_DIM
C_CHUNK = 128
C_CONV = 4
ODD_MAIN = 2 * C_QK_WIDTH + 2 * C_V_WIDTH
D_FF = 2816
PLE_DIM = 256
EPS = 1e-6

LANES = 128
POOL_HALO = 16
CONV_HALO = 8
TM = 512
FF_CHUNK = 704
VMEM_LIMIT = 56 * 1024 * 1024

F32 = jnp.float32
BF16 = jnp.bfloat16


def _rms(x, g):
    ms = jnp.mean(x * x, axis=-1, keepdims=True)
    return x * lax.rsqrt(ms + EPS) * g


def _dot(a, b):
    return jnp.dot(a.astype(BF16), b.astype(BF16), preferred_element_type=F32)


def _sigmoid(x):
    return 1.0 / (1.0 + jnp.exp(-x))


def _silu(x):
    return x * _sigmoid(x)


def _gelu_tanh(x):
    return 0.5 * x * (1.0 + jnp.tanh(0.7978845608028654 * (x + 0.044715 * (x * x * x))))


def _log_sigmoid(x):
    return jnp.minimum(x, 0.0) - jnp.log(1.0 + jnp.exp(-jnp.abs(x)))


def _split3(x):
    hi = x.astype(BF16)
    r = x - hi.astype(F32)
    mid = r.astype(BF16)
    lo = (r - mid.astype(F32)).astype(BF16)
    return hi, mid, lo


def _even_kernel(x_ref, pre_g, w_in, v_gain, ws, bs_t, w_pool, pool_scale, w_out, post_g,
                 o_ref, xb_buf, mix_buf):
    si = pl.program_id(1)
    tm = x_ref.shape[0]
    x = x_ref[...]
    h = _rms(x, pre_g[...]).astype(BF16)
    z = jnp.dot(h, w_in[...], preferred_element_type=F32)
    u = _gelu_tanh(z[:, :A_WIDTH])
    v = _rms(_gelu_tanh(z[:, A_WIDTH:2 * A_WIDTH]), v_gain[...]).astype(BF16)

    @pl.when(si == 0)
    def _():
        xb_buf[0:POOL_HALO, :] = jnp.zeros((POOL_HALO, B_WIDTH), F32)

    @pl.when(si > 0)
    def _():
        xb_buf[0:POOL_HALO, :] = xb_buf[tm:tm + POOL_HALO, :]

    xb_buf[POOL_HALO:POOL_HALO + tm, :] = z[:, 2 * A_WIDTH:]

    row = lax.broadcasted_iota(jnp.int32, (A_CHUNK, A_CHUNK), 0)
    col = lax.broadcasted_iota(jnp.int32, (A_CHUNK, A_CHUNK), 1)
    causal = col <= row
    for hd in range(A_HEADS):
        w_h = jnp.where(causal, ws[hd], 0.0).astype(BF16)
        bias = bs_t[:, hd:hd + 1]
        cs = slice(hd * A_HEAD_DIM, (hd + 1) * A_HEAD_DIM)
        for c in range(tm // A_CHUNK):
            rs = slice(c * A_CHUNK, (c + 1) * A_CHUNK)
            sv = jnp.dot(w_h, v[rs, cs], preferred_element_type=F32) + bias
            mix_buf[rs, cs] = (u[rs, cs] * sv).astype(BF16)

    t_pos = (si * tm + lax.broadcasted_iota(jnp.int32, (tm, 1), 0)).astype(F32)
    for g, win in enumerate(B_WINDOWS):
        cs = slice(g * B_GROUP_DIM, (g + 1) * B_GROUP_DIM)
        cur = xb_buf[POOL_HALO:POOL_HALO + tm, cs]
        acc = cur
        for j in range(1, win):
            acc = acc + xb_buf[POOL_HALO - j:POOL_HALO - j + tm, cs]
        cnt = jnp.minimum(t_pos + 1.0, float(win))
        pooled = acc / cnt - cur
        yb = _dot(pooled, w_pool[g]) * pool_scale[:, cs]
        mix_buf[:, A_WIDTH + g * B_GROUP_DIM:A_WIDTH + (g + 1) * B_GROUP_DIM] = yb.astype(BF16)

    y = jnp.dot(mix_buf[...], w_out[...], preferred_element_type=F32)
    o_ref[...] = x + _rms(y, post_g[...])


def _odd_kernel(x_ref, pre_g, w_main, w_gate, gate_bias, conv_w, h_gain, w_out, post_g,
                o_ref, qk_buf, c_st, n_st, m_st, h_buf):
    si = pl.program_id(1)
    tm = x_ref.shape[0]
    L = C_CHUNK

    @pl.when(si == 0)
    def _():
        qk_buf[0:CONV_HALO, :] = jnp.zeros((CONV_HALO, 2 * C_QK_WIDTH), F32)
        c_st[...] = jnp.zeros(c_st.shape, F32)
        n_st[...] = jnp.zeros(n_st.shape, F32)
        m_st[...] = jnp.zeros(m_st.shape, F32)

    @pl.when(si > 0)
    def _():
        qk_buf[0:CONV_HALO, :] = qk_buf[tm:tm + CONV_HALO, :]

    x = x_ref[...]
    hb = _rms(x, pre_g[...]).astype(BF16)
    z = jnp.dot(hb, w_main[...], preferred_element_type=F32)
    zg = jnp.dot(hb, w_gate[...], preferred_element_type=F32) + gate_bias[...]
    ig_all = zg[:, :LANES]
    lf_all = _log_sigmoid(zg[:, LANES:])

    qk_buf[CONV_HALO:CONV_HALO + tm, :] = z[:, :2 * C_QK_WIDTH]
    conv = qk_buf[CONV_HALO:CONV_HALO + tm, :] * conv_w[0:1, :]
    for k in range(1, C_CONV):
        conv = conv + qk_buf[CONV_HALO - k:CONV_HALO - k + tm, :] * conv_w[k:k + 1, :]
    qk = _silu(conv)
    q_all = (qk[:, :C_QK_WIDTH] * (C_QK_DIM ** -0.5)).astype(BF16)
    k_all = qk[:, C_QK_WIDTH:]
    v_all = z[:, 2 * C_QK_WIDTH:2 * C_QK_WIDTH + C_V_WIDTH].astype(BF16)
    o_all = z[:, 2 * C_QK_WIDTH + C_V_WIDTH:]

    row = lax.broadcasted_iota(jnp.int32, (L, L), 0)
    col = lax.broadcasted_iota(jnp.int32, (L, L), 1)
    causal = col <= row
    tri = jnp.where(causal, 1.0, 0.0).astype(BF16)

    for c in range(tm // L):
        rs = slice(c * L, (c + 1) * L)
        ig_c = ig_all[rs, :]
        hi, mid, lo = _split3(lf_all[rs, :])
        b_c = (jnp.dot(tri, hi, preferred_element_type=F32)
               + jnp.dot(tri, mid, preferred_element_type=F32)
               + jnp.dot(tri, lo, preferred_element_type=F32))
        ig_t = ig_c.T
        b_t = b_c.T
        for hd in range(C_HEADS):
            qs = slice(hd * C_QK_DIM, (hd + 1) * C_QK_DIM)
            vs = slice(hd * C_V_DIM, (hd + 1) * C_V_DIM)
            b_col = b_c[:, hd:hd + 1]
            ig_col = ig_c[:, hd:hd + 1]
            b_row = b_t[hd:hd + 1, :]
            ig_row = ig_t[hd:hd + 1, :]
            m_prev = m_st[hd][:, 0:1]
            c_prev = c_st[hd]
            n_prev = n_st[hd]
            qh = q_all[rs, qs]
            kh = k_all[rs, qs]
            vh = v_all[rs, vs]

            dlog = jnp.where(causal, b_col - b_row + ig_row, -jnp.inf)
            a = b_col + m_prev
            mt = jnp.maximum(a, jnp.max(dlog, axis=-1, keepdims=True))
            wi = jnp.exp(a - mt)
            sc = lax.dot_general(qh, kh.astype(BF16), (((1,), (1,)), ((), ())),
                                 preferred_element_type=F32) * jnp.exp(dlog - mt)
            num = wi * _dot(qh, c_prev) + _dot(sc, vh)
            den = (wi * jnp.sum(qh.astype(F32) * n_prev, axis=-1, keepdims=True)
                   + jnp.sum(sc, axis=-1, keepdims=True))
            hh = num / jnp.maximum(jnp.abs(den), jnp.exp(-mt))

            b_last = b_c[L - 1:L, hd:hd + 1]
            g_col = b_last - b_col + ig_col
            m_new = jnp.maximum(b_last + m_prev, jnp.max(g_col, axis=0, keepdims=True))
            wc = jnp.exp(b_last + m_prev - m_new)
            ks = kh * jnp.exp(g_col - m_new)
            c_st[hd] = wc * c_prev + _dot(ks.T, vh)
            n_st[hd] = wc * n_prev + jnp.sum(ks, axis=0, keepdims=True)
            m_st[hd] = jnp.broadcast_to(m_new, (1, LANES))

            hn = _rms(hh, h_gain[:, vs])
            h_buf[rs, vs] = (hn * _sigmoid(o_all[rs, vs])).astype(BF16)

    y = jnp.dot(h_buf[...], w_out[...], preferred_element_type=F32)
    o_ref[...] = x + _rms(y, post_g[...])


def _ffn_kernel(x_ref, p_ref, pre_g, w_gu, w_down, post_g, w_pgate, w_pproj, ple_g, o_ref):
    x = x_ref[...]
    hn = _rms(x, pre_g[...]).astype(BF16)
    acc = jnp.zeros(x.shape, F32)
    for j in range(D_FF // FF_CHUNK):
        g = jnp.dot(hn, w_gu[:, j * FF_CHUNK:(j + 1) * FF_CHUNK], preferred_element_type=F32)
        u = jnp.dot(hn, w_gu[:, D_FF + j * FF_CHUNK:D_FF + (j + 1) * FF_CHUNK],
                    preferred_element_type=F32)
        act = (_silu(g) * u).astype(BF16)
        acc = acc + jnp.dot(act, w_down[j * FF_CHUNK:(j + 1) * FF_CHUNK, :],
                            preferred_element_type=F32)
    x2 = x + _rms(acc, post_g[...])
    e = _sigmoid(_dot(x2, w_pgate[...])) * _dot(p_ref[...], w_pproj[...])
    o_ref[...] = x2 + _rms(e, ple_g[...])


def _const_spec(shape):
    zeros = (0,) * len(shape)
    return pl.BlockSpec(shape, lambda *_: zeros, pipeline_mode=pl.Buffered(1))


def _row_spec(n_seq_tiles, width):
    return pl.BlockSpec((TM, width), lambda b, s: (b * n_seq_tiles + s, 0))


_PARAMS = pltpu.CompilerParams(dimension_semantics=("arbitrary", "arbitrary"),
                               vmem_limit_bytes=VMEM_LIMIT)


def _even_call(x2d, batch, seq, pre_g, w_in, v_gain, ws, bs_t, w_pool, pool_scale, w_out, post_g):
    ns = seq // TM
    consts = (pre_g, w_in, v_gain, ws, bs_t, w_pool, pool_scale, w_out, post_g)
    return pl.pallas_call(
        _even_kernel,
        out_shape=jax.ShapeDtypeStruct(x2d.shape, F32),
        grid=(batch, ns),
        in_specs=[_row_spec(ns, D_MODEL)] + [_const_spec(a.shape) for a in consts],
        out_specs=_row_spec(ns, D_MODEL),
        scratch_shapes=[pltpu.VMEM((POOL_HALO + TM, B_WIDTH), F32),
                        pltpu.VMEM((TM, EVEN_OUT), BF16)],
        compiler_params=_PARAMS,
        name="even_mixer",
    )(x2d, *consts)


def _odd_call(x2d, batch, seq, pre_g, w_main, w_gate, gate_bias, conv_w, h_gain, w_out, post_g):
    ns = seq // TM
    consts = (pre_g, w_main, w_gate, gate_bias, conv_w, h_gain, w_out, post_g)
    return pl.pallas_call(
        _odd_kernel,
        out_shape=jax.ShapeDtypeStruct(x2d.shape, F32),
        grid=(batch, ns),
        in_specs=[_row_spec(ns, D_MODEL)] + [_const_spec(a.shape) for a in consts],
        out_specs=_row_spec(ns, D_MODEL),
        scratch_shapes=[pltpu.VMEM((CONV_HALO + TM, 2 * C_QK_WIDTH), F32),
                        pltpu.VMEM((C_HEADS, C_QK_DIM, C_V_DIM), F32),
                        pltpu.VMEM((C_HEADS, 1, C_QK_DIM), F32),
                        pltpu.VMEM((C_HEADS, 1, LANES), F32),
                        pltpu.VMEM((TM, C_V_WIDTH), BF16)],
        compiler_params=_PARAMS,
        name="odd_mixer",
    )(x2d, *consts)


def _ffn_call(x2d, p2d, batch, seq, pre_g, w_gu, w_down, post_g, w_pgate, w_pproj, ple_g):
    ns = seq // TM
    consts = (pre_g, w_gu, w_down, post_g, w_pgate, w_pproj, ple_g)
    return pl.pallas_call(
        _ffn_kernel,
        out_shape=jax.ShapeDtypeStruct(x2d.shape, F32),
        grid=(batch, ns),
        in_specs=[_row_spec(ns, D_MODEL), _row_spec(ns, PLE_DIM)]
                 + [_const_spec(a.shape) for a in consts],
        out_specs=_row_spec(ns, D_MODEL),
        compiler_params=_PARAMS,
        name="ffn_ple",
    )(x2d, p2d, *consts)


def kernel(x, p, mix_pre_gain, mix_post_gain, ffn_pre_gain, ffn_post_gain, ple_post_gain, even_w_in, even_a_v_gain, even_a_ws, even_a_bs, even_b_wpool, even_b_scale, even_w_out, odd_w_in, odd_conv_w, odd_b_i, odd_b_f, odd_h_gain, odd_w_out, ffn_w_gate_up, ffn_w_down, ple_proj, ple_gate):
    batch, seq, d = x.shape
    depth = p.shape[0]
    assert d == D_MODEL and seq % TM == 0 and TM % A_CHUNK == 0 and TM % C_CHUNK == 0
    x2d = x.reshape(batch * seq, d)
    row = lambda a: a.reshape(1, -1)
    for i in range(depth):
        j = i // 2
        if i % 2 == 0:
            x2d = _even_call(
                x2d, batch, seq, row(mix_pre_gain[i]), even_w_in[j].astype(BF16),
                row(even_a_v_gain[j]), even_a_ws[j], even_a_bs[j].T, even_b_wpool[j].astype(BF16),
                row(even_b_scale[j]), even_w_out[j].astype(BF16), row(mix_post_gain[i]))
        else:
            w_in = odd_w_in[j]
            w_gate = jnp.zeros((d, 2 * LANES), F32)
            w_gate = w_gate.at[:, :C_HEADS].set(w_in[:, ODD_MAIN:ODD_MAIN + C_HEADS])
            w_gate = w_gate.at[:, LANES:LANES + C_HEADS].set(w_in[:, ODD_MAIN + C_HEADS:])
            gate_bias = jnp.zeros((1, 2 * LANES), F32)
            gate_bias = gate_bias.at[0, :C_HEADS].set(odd_b_i[j])
            gate_bias = gate_bias.at[0, LANES:LANES + C_HEADS].set(odd_b_f[j])
            x2d = _odd_call(
                x2d, batch, seq, row(mix_pre_gain[i]), w_in[:, :ODD_MAIN].astype(BF16),
                w_gate.astype(BF16), gate_bias, odd_conv_w[j], row(odd_h_gain[j]),
                odd_w_out[j].astype(BF16), row(mix_post_gain[i]))
        x2d = _ffn_call(
            x2d, p[i].reshape(batch * seq, PLE_DIM), batch, seq, row(ffn_pre_gain[i]),
            ffn_w_gate_up[i].astype(BF16), ffn_w_down[i].astype(BF16), row(ffn_post_gain[i]),
            ple_gate[i].astype(BF16), ple_proj[i].astype(BF16), row(ple_post_gain[i]))
    return x2d.reshape(batch, seq, d)
```

```python
import functools

import jax
import jax.numpy as jnp
from jax import lax
from jax.experimental import pallas as pl
from jax.experimental.pallas import tpu as pltpu

D_MODEL = 1024
A_HEADS = 4
A_HEAD_DIM = 128
A_WIDTH = A_HEADS * A_HEAD_DIM
A_CHUNK = 128
B_GROUPS = 4
B_GROUP_DIM = 128
B_WIDTH = B_GROUPS * B_GROUP_DIM
B_WINDOWS = (2, 4, 8, 16)
EVEN_IN = 2 * A_WIDTH + B_WIDTH
EVEN_OUT = A_WIDTH + B_WIDTH
C_HEADS = 4
C_QK_DIM = 128
C_V_DIM = 256
C_QK_WIDTH = C_HEADS * C_QK_DIM
C_V_WIDTH = C_HEADS * C_V_DIM
C_CHUNK = 128
C_CONV = 4
ODD_MAIN = 2 * C_QK_WIDTH + 2 * C_V_WIDTH
D_FF = 2816
PLE_DIM = 256
EPS = 1e-6

LANES = 128
POOL_HALO = 16
CONV_HALO = 8
TM = 512
FF_CHUNKS = ((0, 1024), (1024, 1024), (2048, 768))
VMEM_LIMIT = 56 * 1024 * 1024

F32 = jnp.float32
BF16 = jnp.bfloat16


def _rms(x, g):
    ms = jnp.mean(x * x, axis=-1, keepdims=True)
    return x * lax.rsqrt(ms + EPS) * g


def _dot(a, b):
    return jnp.dot(a.astype(BF16), b.astype(BF16), preferred_element_type=F32)


def _sigmoid(x):
    return 1.0 / (1.0 + jnp.exp(-x))


def _silu(x):
    return x * _sigmoid(x)


def _gelu_tanh(x):
    return 0.5 * x * (1.0 + jnp.tanh(0.7978845608028654 * (x + 0.044715 * (x * x * x))))


def _log_sigmoid(x):
    return jnp.minimum(x, 0.0) - jnp.log(1.0 + jnp.exp(-jnp.abs(x)))


def _split3(x):
    hi = x.astype(BF16)
    r = x - hi.astype(F32)
    mid = r.astype(BF16)
    lo = (r - mid.astype(F32)).astype(BF16)
    return hi, mid, lo


def _even_kernel(x_ref, pre_g, w_in, v_gain, ws, bs_t, w_pool, pool_scale, w_out, post_g,
                 o_ref, xb_buf, mix_buf):
    si = pl.program_id(1)
    tm = x_ref.shape[0]
    x = x_ref[...]
    h = _rms(x, pre_g[...]).astype(BF16)
    z = jnp.dot(h, w_in[...], preferred_element_type=F32)
    u = _gelu_tanh(z[:, :A_WIDTH])
    v = _rms(_gelu_tanh(z[:, A_WIDTH:2 * A_WIDTH]), v_gain[...]).astype(BF16)

    @pl.when(si == 0)
    def _():
        xb_buf[0:POOL_HALO, :] = jnp.zeros((POOL_HALO, B_WIDTH), F32)

    @pl.when(si > 0)
    def _():
        xb_buf[0:POOL_HALO, :] = xb_buf[tm:tm + POOL_HALO, :]

    xb_buf[POOL_HALO:POOL_HALO + tm, :] = z[:, 2 * A_WIDTH:]

    row = lax.broadcasted_iota(jnp.int32, (A_CHUNK, A_CHUNK), 0)
    col = lax.broadcasted_iota(jnp.int32, (A_CHUNK, A_CHUNK), 1)
    causal = col <= row
    for hd in range(A_HEADS):
        w_h = jnp.where(causal, ws[hd], 0.0).astype(BF16)
        bias = bs_t[:, hd:hd + 1]
        cs = slice(hd * A_HEAD_DIM, (hd + 1) * A_HEAD_DIM)
        for c in range(tm // A_CHUNK):
            rs = slice(c * A_CHUNK, (c + 1) * A_CHUNK)
            sv = jnp.dot(w_h, v[rs, cs], preferred_element_type=F32) + bias
            mix_buf[rs, cs] = (u[rs, cs] * sv).astype(BF16)

    t_pos = (si * tm + lax.broadcasted_iota(jnp.int32, (tm, 1), 0)).astype(F32)
    for g, win in enumerate(B_WINDOWS):
        cs = slice(g * B_GROUP_DIM, (g + 1) * B_GROUP_DIM)
        cur = xb_buf[POOL_HALO:POOL_HALO + tm, cs]
        acc = cur
        for j in range(1, win):
            acc = acc + xb_buf[POOL_HALO - j:POOL_HALO - j + tm, cs]
        cnt = jnp.minimum(t_pos + 1.0, float(win))
        pooled = acc / cnt - cur
        yb = _dot(pooled, w_pool[g]) * pool_scale[:, cs]
        mix_buf[:, A_WIDTH + g * B_GROUP_DIM:A_WIDTH + (g + 1) * B_GROUP_DIM] = yb.astype(BF16)

    y = jnp.dot(mix_buf[...], w_out[...], preferred_element_type=F32)
    o_ref[...] = x + _rms(y, post_g[...])


def _odd_kernel(x_ref, pre_g, w_main, w_gate, gate_bias, conv_w, h_gain, w_out, post_g,
                o_ref, qk_buf, c_st, n_st, m_st, h_buf):
    si = pl.program_id(1)
    tm = x_ref.shape[0]
    L = C_CHUNK

    @pl.when(si == 0)
    def _():
        qk_buf[0:CONV_HALO, :] = jnp.zeros((CONV_HALO, 2 * C_QK_WIDTH), F32)
        c_st[...] = jnp.zeros(c_st.shape, F32)
        n_st[...] = jnp.zeros(n_st.shape, F32)
        m_st[...] = jnp.zeros(m_st.shape, F32)

    @pl.when(si > 0)
    def _():
        qk_buf[0:CONV_HALO, :] = qk_buf[tm:tm + CONV_HALO, :]

    x = x_ref[...]
    hb = _rms(x, pre_g[...]).astype(BF16)
    z = jnp.dot(hb, w_main[...], preferred_element_type=F32)
    zg = jnp.dot(hb, w_gate[...], preferred_element_type=F32) + gate_bias[...]
    ig_all = zg[:, :LANES]
    lf_all = _log_sigmoid(zg[:, LANES:])

    qk_buf[CONV_HALO:CONV_HALO + tm, :] = z[:, :2 * C_QK_WIDTH]
    conv = qk_buf[CONV_HALO:CONV_HALO + tm, :] * conv_w[0:1, :]
    for k in range(1, C_CONV):
        conv = conv + qk_buf[CONV_HALO - k:CONV_HALO - k + tm, :] * conv_w[k:k + 1, :]
    qk = _silu(conv)
    q_all = (qk[:, :C_QK_WIDTH] * (C_QK_DIM ** -0.5)).astype(BF16)
    k_all = qk[:, C_QK_WIDTH:]
    v_all = z[:, 2 * C_QK_WIDTH:2 * C_QK_WIDTH + C_V_WIDTH].astype(BF16)
    o_all = z[:, 2 * C_QK_WIDTH + C_V_WIDTH:]

    row = lax.broadcasted_iota(jnp.int32, (L, L), 0)
    col = lax.broadcasted_iota(jnp.int32, (L, L), 1)
    causal = col <= row
    tri = jnp.where(causal, 1.0, 0.0).astype(BF16)

    for c in range(tm // L):
        rs = slice(c * L, (c + 1) * L)
        ig_c = ig_all[rs, :]
        hi, mid, lo = _split3(lf_all[rs, :])
        b_c = (jnp.dot(tri, hi, preferred_element_type=F32)
               + jnp.dot(tri, mid, preferred_element_type=F32)
               + jnp.dot(tri, lo, preferred_element_type=F32))
        ig_t = ig_c.T
        b_t = b_c.T
        for hd in range(C_HEADS):
            qs = slice(hd * C_QK_DIM, (hd + 1) * C_QK_DIM)
            vs = slice(hd * C_V_DIM, (hd + 1) * C_V_DIM)
            b_col = b_c[:, hd:hd + 1]
            ig_col = ig_c[:, hd:hd + 1]
            b_row = b_t[hd:hd + 1, :]
            ig_row = ig_t[hd:hd + 1, :]
            m_prev = m_st[hd][:, 0:1]
            c_prev = c_st[hd]
            n_prev = n_st[hd]
            qh = q_all[rs, qs]
            kh = k_all[rs, qs]
            vh = v_all[rs, vs]

            dlog = jnp.where(causal, b_col - b_row + ig_row, -jnp.inf)
            a = b_col + m_prev
            mt = jnp.maximum(a, jnp.max(dlog, axis=-1, keepdims=True))
            wi = jnp.exp(a - mt)
            sc = lax.dot_general(qh, kh.astype(BF16), (((1,), (1,)), ((), ())),
                                 preferred_element_type=F32) * jnp.exp(dlog - mt)
            num = wi * _dot(qh, c_prev) + _dot(sc, vh)
            den = (wi * jnp.sum(qh.astype(F32) * n_prev, axis=-1, keepdims=True)
                   + jnp.sum(sc, axis=-1, keepdims=True))
            hh = num / jnp.maximum(jnp.abs(den), jnp.exp(-mt))

            b_last = b_c[L - 1:L, hd:hd + 1]
            g_col = b_last - b_col + ig_col
            m_new = jnp.maximum(b_last + m_prev, jnp.max(g_col, axis=0, keepdims=True))
            wc = jnp.exp(b_last + m_prev - m_new)
            ks = kh * jnp.exp(g_col - m_new)
            c_st[hd] = wc * c_prev + _dot(ks.T, vh)
            n_st[hd] = wc * n_prev + jnp.sum(ks, axis=0, keepdims=True)
            m_st[hd] = jnp.broadcast_to(m_new, (1, LANES))

            hn = _rms(hh, h_gain[:, vs])
            h_buf[rs, vs] = (hn * _sigmoid(o_all[rs, vs])).astype(BF16)

    y = jnp.dot(h_buf[...], w_out[...], preferred_element_type=F32)
    o_ref[...] = x + _rms(y, post_g[...])


def _ffn_kernel(x_ref, p_ref, pre_g, w_gu, w_down, post_g, w_pgate, w_pproj, ple_g, o_ref):
    x = x_ref[...]
    hn = _rms(x, pre_g[...]).astype(BF16)
    acc = jnp.zeros(x.shape, F32)
    for start, size in FF_CHUNKS:
        g = jnp.dot(hn, w_gu[:, start:start + size], preferred_element_type=F32)
        u = jnp.dot(hn, w_gu[:, D_FF + start:D_FF + start + size], preferred_element_type=F32)
        act = (_silu(g) * u).astype(BF16)
        acc = acc + jnp.dot(act, w_down[start:start + size, :], preferred_element_type=F32)
    x2 = x + _rms(acc, post_g[...])
    e = _sigmoid(_dot(x2, w_pgate[...])) * _dot(p_ref[...], w_pproj[...])
    o_ref[...] = x2 + _rms(e, ple_g[...])


def _layer_spec(stacked, layer):
    rest = stacked.shape[1:]
    zeros = (0,) * len(rest)
    return pl.BlockSpec((None,) + rest, lambda *_: (layer,) + zeros, pipeline_mode=pl.Buffered(1))


def _row_spec(n_seq_tiles, width):
    return pl.BlockSpec((TM, width), lambda b, s: (b * n_seq_tiles + s, 0))


_PARAMS = pltpu.CompilerParams(dimension_semantics=("arbitrary", "arbitrary"),
                               vmem_limit_bytes=VMEM_LIMIT)


def _even_call(x2d, batch, seq, layers, stacked):
    ns = seq // TM
    return pl.pallas_call(
        _even_kernel,
        out_shape=jax.ShapeDtypeStruct(x2d.shape, F32),
        grid=(batch, ns),
        in_specs=[_row_spec(ns, D_MODEL)] + [_layer_spec(a, l) for a, l in zip(stacked, layers)],
        out_specs=_row_spec(ns, D_MODEL),
        scratch_shapes=[pltpu.VMEM((POOL_HALO + TM, B_WIDTH), F32),
                        pltpu.VMEM((TM, EVEN_OUT), BF16)],
        compiler_params=_PARAMS,
        name="even_mixer",
    )(x2d, *stacked)


def _odd_call(x2d, batch, seq, layers, stacked):
    ns = seq // TM
    return pl.pallas_call(
        _odd_kernel,
        out_shape=jax.ShapeDtypeStruct(x2d.shape, F32),
        grid=(batch, ns),
        in_specs=[_row_spec(ns, D_MODEL)] + [_layer_spec(a, l) for a, l in zip(stacked, layers)],
        out_specs=_row_spec(ns, D_MODEL),
        scratch_shapes=[pltpu.VMEM((CONV_HALO + TM, 2 * C_QK_WIDTH), F32),
                        pltpu.VMEM((C_HEADS, C_QK_DIM, C_V_DIM), F32),
                        pltpu.VMEM((C_HEADS, 1, C_QK_DIM), F32),
                        pltpu.VMEM((C_HEADS, 1, LANES), F32),
                        pltpu.VMEM((TM, C_V_WIDTH), BF16)],
        compiler_params=_PARAMS,
        name="odd_mixer",
    )(x2d, *stacked)


def _ffn_call(x2d, p3d, batch, seq, layer, stacked):
    ns = seq // TM
    p_spec = pl.BlockSpec((None, TM, PLE_DIM), lambda b, s: (layer, b * ns + s, 0))
    return pl.pallas_call(
        _ffn_kernel,
        out_shape=jax.ShapeDtypeStruct(x2d.shape, F32),
        grid=(batch, ns),
        in_specs=[_row_spec(ns, D_MODEL), p_spec] + [_layer_spec(a, layer) for a in stacked],
        out_specs=_row_spec(ns, D_MODEL),
        compiler_params=_PARAMS,
        name="ffn_ple",
    )(x2d, p3d, *stacked)


def kernel(x, p, mix_pre_gain, mix_post_gain, ffn_pre_gain, ffn_post_gain, ple_post_gain, even_w_in, even_a_v_gain, even_a_ws, even_a_bs, even_b_wpool, even_b_scale, even_w_out, odd_w_in, odd_conv_w, odd_b_i, odd_b_f, odd_h_gain, odd_w_out, ffn_w_gate_up, ffn_w_down, ple_proj, ple_gate):
    batch, seq, d = x.shape
    depth = p.shape[0]
    assert d == D_MODEL and seq % TM == 0 and TM % A_CHUNK == 0 and TM % C_CHUNK == 0
    rows = lambda a: a[:, None, :]

    mix_pre, mix_post = rows(mix_pre_gain), rows(mix_post_gain)
    even_stacked = (mix_pre, even_w_in.astype(BF16), rows(even_a_v_gain), even_a_ws,
                    jnp.swapaxes(even_a_bs, 1, 2), even_b_wpool.astype(BF16), rows(even_b_scale),
                    even_w_out.astype(BF16), mix_post)
    n_odd = odd_w_in.shape[0]
    w_gate = jnp.zeros((n_odd, d, 2 * LANES), F32)
    w_gate = w_gate.at[:, :, :C_HEADS].set(odd_w_in[:, :, ODD_MAIN:ODD_MAIN + C_HEADS])
    w_gate = w_gate.at[:, :, LANES:LANES + C_HEADS].set(odd_w_in[:, :, ODD_MAIN + C_HEADS:])
    gate_bias = jnp.zeros((n_odd, 1, 2 * LANES), F32)
    gate_bias = gate_bias.at[:, 0, :C_HEADS].set(odd_b_i)
    gate_bias = gate_bias.at[:, 0, LANES:LANES + C_HEADS].set(odd_b_f)
    odd_stacked = (mix_pre, odd_w_in[:, :, :ODD_MAIN].astype(BF16), w_gate.astype(BF16), gate_bias,
                   odd_conv_w, rows(odd_h_gain), odd_w_out.astype(BF16), mix_post)
    ffn_stacked = (rows(ffn_pre_gain), ffn_w_gate_up.astype(BF16), ffn_w_down.astype(BF16),
                   rows(ffn_post_gain), ple_gate.astype(BF16), ple_proj.astype(BF16),
                   rows(ple_post_gain))

    x2d = x.reshape(batch * seq, d)
    p3d = p.reshape(depth, batch * seq, PLE_DIM)
    for i in range(depth):
        j = i // 2
        if i % 2 == 0:
            x2d = _even_call(x2d, batch, seq, (i, j, j, j, j, j, j, j, i), even_stacked)
        else:
            x2d = _odd_call(x2d, batch, seq, (i, j, j, j, j, j, j, i), odd_stacked)
        x2d = _ffn_call(x2d, p3d, batch, seq, i, ffn_stacked)
    return x2d.reshape(batch, seq, d)
```

```python
import functools

import jax
import jax.numpy as jnp
from jax import lax
from jax.experimental import pallas as pl
from jax.experimental.pallas import tpu as pltpu

D_MODEL = 1024
A_HEADS = 4
A_HEAD_DIM = 128
A_WIDTH = A_HEADS * A_HEAD_DIM
A_CHUNK = 128
B_GROUPS = 4
B_GROUP_DIM = 128
B_WIDTH = B_GROUPS * B_GROUP_DIM
B_WINDOWS = (2, 4, 8, 16)
EVEN_IN = 2 * A_WIDTH + B_WIDTH
EVEN_OUT = A_WIDTH + B_WIDTH
C_HEADS = 4
C_QK_DIM = 128
C_V_DIM = 256
C_QK_WIDTH = C_HEADS * C_QK_DIM
C_V_WIDTH = C_HEADS * C_V_DIM
C_CHUNK = 128
C_CONV = 4
ODD_MAIN = 2 * C_QK_WIDTH + 2 * C_V_WIDTH
D_FF = 2816
PLE_DIM = 256
EPS = 1e-6

LANES = 128
POOL_HALO = 16
CONV_HALO = 8
TM = 512
FF_CHUNKS = ((0, 1024), (1024, 1024), (2048, 768))
PIECE = 256
VMEM_LIMIT = 56 * 1024 * 1024

F32 = jnp.float32
BF16 = jnp.bfloat16


def _rms(x, g):
    ms = jnp.mean(x * x, axis=-1, keepdims=True)
    return x * lax.rsqrt(ms + EPS) * g


def _dot(a, b):
    return jnp.dot(a.astype(BF16), b.astype(BF16), preferred_element_type=F32)


def _sigmoid(x):
    return 1.0 / (1.0 + jnp.exp(-x))


def _silu(x):
    return x * _sigmoid(x)


def _gelu_tanh(x):
    return 0.5 * x * (1.0 + jnp.tanh(0.7978845608028654 * (x + 0.044715 * (x * x * x))))


def _log_sigmoid(x):
    return jnp.minimum(x, 0.0) - jnp.log(1.0 + jnp.exp(-jnp.abs(x)))


def _split3(x):
    hi = x.astype(BF16)
    r = x - hi.astype(F32)
    mid = r.astype(BF16)
    lo = (r - mid.astype(F32)).astype(BF16)
    return hi, mid, lo


def _even_kernel(x_ref, pre_g, w_in, v_gain, ws, bs_t, w_pool, pool_scale, w_out, post_g,
                 o_ref, xb_buf, mix_buf):
    si = pl.program_id(1)
    tm = x_ref.shape[0]
    x = x_ref[...]
    h = _rms(x, pre_g[...]).astype(BF16)
    z = jnp.dot(h, w_in[...], preferred_element_type=F32)
    u = _gelu_tanh(z[:, :A_WIDTH])
    v = _rms(_gelu_tanh(z[:, A_WIDTH:2 * A_WIDTH]), v_gain[...]).astype(BF16)

    @pl.when(si == 0)
    def _():
        xb_buf[0:POOL_HALO, :] = jnp.zeros((POOL_HALO, B_WIDTH), F32)

    @pl.when(si > 0)
    def _():
        xb_buf[0:POOL_HALO, :] = xb_buf[tm:tm + POOL_HALO, :]

    xb_buf[POOL_HALO:POOL_HALO + tm, :] = z[:, 2 * A_WIDTH:]

    row = lax.broadcasted_iota(jnp.int32, (A_CHUNK, A_CHUNK), 0)
    col = lax.broadcasted_iota(jnp.int32, (A_CHUNK, A_CHUNK), 1)
    causal = col <= row
    for hd in range(A_HEADS):
        w_h = jnp.where(causal, ws[hd], 0.0).astype(BF16)
        bias = bs_t[:, hd:hd + 1]
        cs = slice(hd * A_HEAD_DIM, (hd + 1) * A_HEAD_DIM)
        for c in range(tm // A_CHUNK):
            rs = slice(c * A_CHUNK, (c + 1) * A_CHUNK)
            sv = jnp.dot(w_h, v[rs, cs], preferred_element_type=F32) + bias
            mix_buf[rs, cs] = (u[rs, cs] * sv).astype(BF16)

    t_pos = (si * tm + lax.broadcasted_iota(jnp.int32, (tm, 1), 0)).astype(F32)
    for g, win in enumerate(B_WINDOWS):
        cs = slice(g * B_GROUP_DIM, (g + 1) * B_GROUP_DIM)
        cur = xb_buf[POOL_HALO:POOL_HALO + tm, cs]
        acc = cur
        for j in range(1, win):
            acc = acc + xb_buf[POOL_HALO - j:POOL_HALO - j + tm, cs]
        cnt = jnp.minimum(t_pos + 1.0, float(win))
        pooled = acc / cnt - cur
        yb = _dot(pooled, w_pool[g]) * pool_scale[:, cs]
        mix_buf[:, A_WIDTH + g * B_GROUP_DIM:A_WIDTH + (g + 1) * B_GROUP_DIM] = yb.astype(BF16)

    y = jnp.dot(mix_buf[...], w_out[...], preferred_element_type=F32)
    o_ref[...] = x + _rms(y, post_g[...])


def _odd_step(t, n_seq_tiles, xc_ref, xo_ref, pre_g, w_main, w_gate, gate_bias, conv_w, h_gain,
              w_out, post_g, o_ref, cur, prev, h_new, h_old, c_st, n_st, m_st, y_buf):
    zq_c, zv_c, zo_c, zg_c = cur
    zq_p, zv_p, zo_p, zg_p = prev
    tm = xc_ref.shape[0]
    L = C_CHUNK
    seq_start = (jnp.maximum(t - 1, 0) % n_seq_tiles) == 0

    @pl.when(seq_start)
    def _():
        zq_p[0:CONV_HALO, :] = jnp.zeros((CONV_HALO, 2 * C_QK_WIDTH), F32)
        c_st[...] = jnp.zeros(c_st.shape, F32)
        n_st[...] = jnp.zeros(n_st.shape, F32)
        m_st[...] = jnp.zeros(m_st.shape, F32)

    @pl.when(jnp.logical_not(seq_start))
    def _():
        zq_p[0:CONV_HALO, :] = zq_c[tm:tm + CONV_HALO, :]

    hb = _rms(xc_ref[...], pre_g[...]).astype(BF16)

    def project_piece(j):
        cols = slice(j * PIECE, (j + 1) * PIECE)
        z = jnp.dot(hb, w_main[:, cols], preferred_element_type=F32)
        if cols.stop <= 2 * C_QK_WIDTH:
            zq_c[CONV_HALO:CONV_HALO + tm, cols] = z
        elif cols.stop <= 2 * C_QK_WIDTH + C_V_WIDTH:
            zv_c[:, cols.start - 2 * C_QK_WIDTH:cols.stop - 2 * C_QK_WIDTH] = z.astype(BF16)
        else:
            off = 2 * C_QK_WIDTH + C_V_WIDTH
            zo_c[:, cols.start - off:cols.stop - off] = z

    def project_gates():
        zg_c[...] = jnp.dot(hb, w_gate[...], preferred_element_type=F32) + gate_bias[...]

    def out_piece(j):
        cols = slice(j * PIECE, (j + 1) * PIECE)
        y_buf[:, cols] = jnp.dot(h_old[...], w_out[:, cols], preferred_element_type=F32)

    qk_pieces = 2 * C_QK_WIDTH // PIECE
    v_pieces = C_V_WIDTH // PIECE
    fill_conv = [functools.partial(project_piece, j) for j in range(qk_pieces)]
    fill_intra = ([functools.partial(project_piece, qk_pieces + j) for j in range(v_pieces)]
                  + [project_gates])
    fill_rec = ([functools.partial(project_piece, qk_pieces + v_pieces + j) for j in range(v_pieces)]
                + [functools.partial(out_piece, j) for j in range(D_MODEL // PIECE)])

    qk = []
    for j in range(qk_pieces):
        cols = slice(j * PIECE, (j + 1) * PIECE)
        conv = zq_p[CONV_HALO:CONV_HALO + tm, cols] * conv_w[0:1, cols]
        for k in range(1, C_CONV):
            conv = conv + zq_p[CONV_HALO - k:CONV_HALO - k + tm, cols] * conv_w[k:k + 1, cols]
        qk.append(_silu(conv))
        fill_conv[j]()
    heads_per_piece = PIECE // C_QK_DIM

    def head_cols(pieces, hd):
        lo_col = (hd % heads_per_piece) * C_QK_DIM
        return pieces[hd // heads_per_piece][:, lo_col:lo_col + C_QK_DIM]

    q_pieces = [a * (C_QK_DIM ** -0.5) for a in qk[:qk_pieces // 2]]
    k_pieces = qk[qk_pieces // 2:]
    q_of = [head_cols(q_pieces, hd) for hd in range(C_HEADS)]
    k_of = [head_cols(k_pieces, hd) for hd in range(C_HEADS)]
    qb_of = [a.astype(BF16) for a in q_of]
    kb_of = [a.astype(BF16) for a in k_of]

    row = lax.broadcasted_iota(jnp.int32, (L, L), 0)
    col = lax.broadcasted_iota(jnp.int32, (L, L), 1)
    causal = col <= row
    tri = jnp.where(causal, 1.0, 0.0).astype(BF16)

    n_chunks = tm // L
    rows_of = lambda c: slice(c * L, (c + 1) * L)
    v_of = lambda hd: slice(hd * C_V_DIM, (hd + 1) * C_V_DIM)
    pairs = [(c, hd) for c in range(n_chunks) for hd in range(C_HEADS)]

    ig_all = zg_p[:, :LANES]
    hi, mid, lo = _split3(_log_sigmoid(zg_p[:, LANES:]))
    gates = []
    for c in range(n_chunks):
        rs = rows_of(c)
        b_c = (jnp.dot(tri, hi[rs], preferred_element_type=F32)
               + jnp.dot(tri, mid[rs], preferred_element_type=F32)
               + jnp.dot(tri, lo[rs], preferred_element_type=F32))
        gates.append((ig_all[rs], b_c, ig_all[rs].T, b_c.T))
    qk_t = {(c, hd): lax.dot_general(qb_of[hd][rows_of(c)], kb_of[hd][rows_of(c)],
                                     (((1,), (1,)), ((), ())), preferred_element_type=F32)
            for c, hd in pairs}
    intra = {}
    for i, (c, hd) in enumerate(pairs):
        if i % 3 == 0 and i // 3 < len(fill_intra):
            fill_intra[i // 3]()
        rs = rows_of(c)
        ig_c, b_c, ig_t, b_t = gates[c]
        b_col = b_c[:, hd:hd + 1]
        ig_col = ig_c[:, hd:hd + 1]
        vh = zv_p[rs, v_of(hd)]
        dlog = jnp.where(causal, b_col - b_t[hd:hd + 1, :] + ig_t[hd:hd + 1, :], -jnp.inf)
        mt_i = jnp.max(dlog, axis=-1, keepdims=True)
        sc = qk_t[c, hd] * jnp.exp(dlog - mt_i)
        sv = jnp.dot(sc.astype(BF16), vh, preferred_element_type=F32)
        s_sum = jnp.sum(sc, axis=-1, keepdims=True)
        b_last = b_c[L - 1:L, hd:hd + 1]
        g_col = b_last - b_col + ig_col
        mg = jnp.max(g_col, axis=0, keepdims=True)
        ks = k_of[hd][rs] * jnp.exp(g_col - mg)
        kv = _dot(ks.T, vh)
        k_sum = jnp.sum(ks, axis=0, keepdims=True)
        intra[c, hd] = (b_col, b_last, mt_i, sv, s_sum, mg, kv, k_sum)

    state = [(c_st[hd], n_st[hd], m_st[hd][:, 0:1]) for hd in range(C_HEADS)]
    for i, (c, hd) in enumerate(pairs):
        if i % 2 == 0:
            fill_rec[i // 2]()
        rs = rows_of(c)
        c_prev, n_prev, m_prev = state[hd]
        b_col, b_last, mt_i, sv, s_sum, mg, kv, k_sum = intra[c, hd]
        a = b_col + m_prev
        mt = jnp.maximum(a, mt_i)
        wi = jnp.exp(a - mt)
        wn = jnp.exp(mt_i - mt)
        num = wi * _dot(qb_of[hd][rs], c_prev) + wn * sv
        den = wi * jnp.sum(q_of[hd][rs] * n_prev, axis=-1, keepdims=True) + wn * s_sum
        hh = num / jnp.maximum(jnp.abs(den), jnp.exp(-mt))
        m_new = jnp.maximum(b_last + m_prev, mg)
        wc = jnp.exp(b_last + m_prev - m_new)
        wk = jnp.exp(mg - m_new)
        state[hd] = (wc * c_prev + wk * kv, wc * n_prev + wk * k_sum, m_new)
        hn = _rms(hh, h_gain[:, v_of(hd)])
        h_new[rs, v_of(hd)] = (hn * _sigmoid(zo_p[rs, v_of(hd)])).astype(BF16)
    for hd in range(C_HEADS):
        c_st[hd], n_st[hd] = state[hd][0], state[hd][1]
        m_st[hd] = jnp.broadcast_to(state[hd][2], (1, LANES))

    o_ref[...] = xo_ref[...] + _rms(y_buf[...], post_g[...])


def _odd_kernel(xc_ref, xo_ref, pre_g, w_main, w_gate, gate_bias, conv_w, h_gain, w_out, post_g,
                o_ref, zq0, zq1, zv0, zv1, zo0, zo1, zg0, zg1, h0, h1, c_st, n_st, m_st, y_buf,
                *, n_seq_tiles):
    t = pl.program_id(0)
    slot0 = (zq0, zv0, zo0, zg0)
    slot1 = (zq1, zv1, zo1, zg1)

    @pl.when(t == 0)
    def _():
        h0[...] = jnp.zeros(h0.shape, BF16)
        zq1[...] = jnp.zeros(zq1.shape, F32)
        zv1[...] = jnp.zeros(zv1.shape, BF16)
        zo1[...] = jnp.zeros(zo1.shape, F32)
        zg1[...] = jnp.zeros(zg1.shape, F32)

    step = functools.partial(
        _odd_step, t, n_seq_tiles, xc_ref, xo_ref, pre_g, w_main, w_gate, gate_bias, conv_w,
        h_gain, w_out, post_g, o_ref)

    @pl.when(t % 2 == 0)
    def _():
        step(slot0, slot1, h1, h0, c_st, n_st, m_st, y_buf)

    @pl.when(t % 2 == 1)
    def _():
        step(slot1, slot0, h0, h1, c_st, n_st, m_st, y_buf)


def _ffn_kernel(x_ref, p_ref, pre_g, w_gu, w_down, post_g, w_pgate, w_pproj, ple_g, o_ref):
    x = x_ref[...]
    hn = _rms(x, pre_g[...]).astype(BF16)
    acc = jnp.zeros(x.shape, F32)
    for start, size in FF_CHUNKS:
        g = jnp.dot(hn, w_gu[:, start:start + size], preferred_element_type=F32)
        u = jnp.dot(hn, w_gu[:, D_FF + start:D_FF + start + size], preferred_element_type=F32)
        act = (_silu(g) * u).astype(BF16)
        acc = acc + jnp.dot(act, w_down[start:start + size, :], preferred_element_type=F32)
    x2 = x + _rms(acc, post_g[...])
    e = _sigmoid(_dot(x2, w_pgate[...])) * _dot(p_ref[...], w_pproj[...])
    o_ref[...] = x2 + _rms(e, ple_g[...])


def _layer_spec(stacked, layer):
    rest = stacked.shape[1:]
    zeros = (0,) * len(rest)
    return pl.BlockSpec((None,) + rest, lambda *_: (layer,) + zeros, pipeline_mode=pl.Buffered(1))


def _row_spec(n_seq_tiles, width):
    return pl.BlockSpec((TM, width), lambda b, s: (b * n_seq_tiles + s, 0))


_PARAMS = pltpu.CompilerParams(dimension_semantics=("arbitrary", "arbitrary"),
                               vmem_limit_bytes=VMEM_LIMIT)


def _even_call(x2d, batch, seq, layers, stacked):
    ns = seq // TM
    return pl.pallas_call(
        _even_kernel,
        out_shape=jax.ShapeDtypeStruct(x2d.shape, F32),
        grid=(batch, ns),
        in_specs=[_row_spec(ns, D_MODEL)] + [_layer_spec(a, l) for a, l in zip(stacked, layers)],
        out_specs=_row_spec(ns, D_MODEL),
        scratch_shapes=[pltpu.VMEM((POOL_HALO + TM, B_WIDTH), F32),
                        pltpu.VMEM((TM, EVEN_OUT), BF16)],
        compiler_params=_PARAMS,
        name="even_mixer",
    )(x2d, *stacked)


def _odd_call(x2d, batch, seq, layers, stacked):
    ns = seq // TM
    n_tiles = batch * ns
    cur_spec = pl.BlockSpec((TM, D_MODEL), lambda t: (jnp.minimum(t, n_tiles - 1), 0))
    old_spec = pl.BlockSpec((TM, D_MODEL), lambda t: (jnp.maximum(t - 2, 0), 0))
    slot = lambda width, dtype, halo=0: [pltpu.VMEM((halo + TM, width), dtype)] * 2
    return pl.pallas_call(
        functools.partial(_odd_kernel, n_seq_tiles=ns),
        out_shape=jax.ShapeDtypeStruct(x2d.shape, F32),
        grid=(n_tiles + 2,),
        in_specs=[cur_spec, old_spec] + [_layer_spec(a, l) for a, l in zip(stacked, layers)],
        out_specs=old_spec,
        scratch_shapes=(slot(2 * C_QK_WIDTH, F32, CONV_HALO) + slot(C_V_WIDTH, BF16)
                        + slot(C_V_WIDTH, F32) + slot(2 * LANES, F32) + slot(C_V_WIDTH, BF16)
                        + [pltpu.VMEM((C_HEADS, C_QK_DIM, C_V_DIM), F32),
                           pltpu.VMEM((C_HEADS, 1, C_QK_DIM), F32),
                           pltpu.VMEM((C_HEADS, 1, LANES), F32),
                           pltpu.VMEM((TM, D_MODEL), F32)]),
        compiler_params=pltpu.CompilerParams(dimension_semantics=("arbitrary",),
                                             vmem_limit_bytes=VMEM_LIMIT),
        name="odd_mixer",
    )(x2d, x2d, *stacked)


def _ffn_call(x2d, p3d, batch, seq, layer, stacked):
    ns = seq // TM
    p_spec = pl.BlockSpec((None, TM, PLE_DIM), lambda b, s: (layer, b * ns + s, 0))
    return pl.pallas_call(
        _ffn_kernel,
        out_shape=jax.ShapeDtypeStruct(x2d.shape, F32),
        grid=(batch, ns),
        in_specs=[_row_spec(ns, D_MODEL), p_spec] + [_layer_spec(a, layer) for a in stacked],
        out_specs=_row_spec(ns, D_MODEL),
        compiler_params=_PARAMS,
        name="ffn_ple",
    )(x2d, p3d, *stacked)


def kernel(x, p, mix_pre_gain, mix_post_gain, ffn_pre_gain, ffn_post_gain, ple_post_gain, even_w_in, even_a_v_gain, even_a_ws, even_a_bs, even_b_wpool, even_b_scale, even_w_out, odd_w_in, odd_conv_w, odd_b_i, odd_b_f, odd_h_gain, odd_w_out, ffn_w_gate_up, ffn_w_down, ple_proj, ple_gate):
    batch, seq, d = x.shape
    depth = p.shape[0]
    assert d == D_MODEL and seq % TM == 0 and TM % A_CHUNK == 0 and TM % C_CHUNK == 0
    rows = lambda a: a[:, None, :]

    mix_pre, mix_post = rows(mix_pre_gain), rows(mix_post_gain)
    even_stacked = (mix_pre, even_w_in.astype(BF16), rows(even_a_v_gain), even_a_ws,
                    jnp.swapaxes(even_a_bs, 1, 2), even_b_wpool.astype(BF16), rows(even_b_scale),
                    even_w_out.astype(BF16), mix_post)
    n_odd = odd_w_in.shape[0]
    w_gate = jnp.zeros((n_odd, d, 2 * LANES), F32)
    w_gate = w_gate.at[:, :, :C_HEADS].set(odd_w_in[:, :, ODD_MAIN:ODD_MAIN + C_HEADS])
    w_gate = w_gate.at[:, :, LANES:LANES + C_HEADS].set(odd_w_in[:, :, ODD_MAIN + C_HEADS:])
    gate_bias = jnp.zeros((n_odd, 1, 2 * LANES), F32)
    gate_bias = gate_bias.at[:, 0, :C_HEADS].set(odd_b_i)
    gate_bias = gate_bias.at[:, 0, LANES:LANES + C_HEADS].set(odd_b_f)
    odd_stacked = (mix_pre, odd_w_in[:, :, :ODD_MAIN].astype(BF16), w_gate.astype(BF16), gate_bias,
                   odd_conv_w, rows(odd_h_gain), odd_w_out.astype(BF16), mix_post)
    ffn_stacked = (rows(ffn_pre_gain), ffn_w_gate_up.astype(BF16), ffn_w_down.astype(BF16),
                   rows(ffn_post_gain), ple_gate.astype(BF16), ple_proj.astype(BF16),
                   rows(ple_post_gain))

    x2d = x.reshape(batch * seq, d)
    p3d = p.reshape(depth, batch * seq, PLE_DIM)
    for i in range(depth):
        j = i // 2
        if i % 2 == 0:
            x2d = _even_call(x2d, batch, seq, (i, j, j, j, j, j, j, j, i), even_stacked)
        else:
            x2d = _odd_call(x2d, batch, seq, (i, j, j, j, j, j, j, i), odd_stacked)
        x2d = _ffn_call(x2d, p3d, batch, seq, i, ffn_stacked)
    return x2d.reshape(batch, seq, d)
```

```python
import functools

import jax
import jax.numpy as jnp
from jax import lax
from jax.experimental import pallas as pl
from jax.experimental.pallas import tpu as pltpu

D_MODEL = 1024
A_HEADS = 4
A_HEAD_DIM = 128
A_WIDTH = A_HEADS * A_HEAD_DIM
A_CHUNK = 128
B_GROUPS = 4
B_GROUP_DIM = 128
B_WIDTH = B_GROUPS * B_GROUP_DIM
B_WINDOWS = (2, 4, 8, 16)
EVEN_IN = 2 * A_WIDTH + B_WIDTH
EVEN_OUT = A_WIDTH + B_WIDTH
C_HEADS = 4
C_QK_DIM = 128
C_V_DIM = 256
C_QK_WIDTH = C_HEADS * C_QK_DIM
C_V_WIDTH = C_HEADS * C_V_DIM
C_CHUNK = 128
C_CONV = 4
ODD_MAIN = 2 * C_QK_WIDTH + 2 * C_V_WIDTH
D_FF = 2816
PLE_DIM = 256
EPS = 1e-6

LANES = 128
POOL_HALO = 16
CONV_HALO = 8
TM = 512
FF_CHUNKS = ((0, 1024), (1024, 1024), (2048, 768))
PIECE = 256
VMEM_LIMIT = 56 * 1024 * 1024

F32 = jnp.float32
BF16 = jnp.bfloat16


def _rms(x, g):
    ms = jnp.mean(x * x, axis=-1, keepdims=True)
    return x * lax.rsqrt(ms + EPS) * g


def _dot(a, b):
    return jnp.dot(a.astype(BF16), b.astype(BF16), preferred_element_type=F32)


def _sigmoid(x):
    return 1.0 / (1.0 + jnp.exp(-x))


def _silu(x):
    return x * _sigmoid(x)


def _gelu_tanh(x):
    return 0.5 * x * (1.0 + jnp.tanh(0.7978845608028654 * (x + 0.044715 * (x * x * x))))


def _log_sigmoid(x):
    return jnp.minimum(x, 0.0) - jnp.log(1.0 + jnp.exp(-jnp.abs(x)))


def _split3(x):
    hi = x.astype(BF16)
    r = x - hi.astype(F32)
    mid = r.astype(BF16)
    lo = (r - mid.astype(F32)).astype(BF16)
    return hi, mid, lo


def _even_kernel(x_ref, pre_g, w_in, v_gain, ws, bs_t, w_pool, pool_scale, w_out, post_g,
                 o_ref, xb_buf, mix_buf):
    si = pl.program_id(1)
    tm = x_ref.shape[0]
    x = x_ref[...]
    h = _rms(x, pre_g[...]).astype(BF16)
    z = jnp.dot(h, w_in[...], preferred_element_type=F32)
    u = _gelu_tanh(z[:, :A_WIDTH])
    v = _rms(_gelu_tanh(z[:, A_WIDTH:2 * A_WIDTH]), v_gain[...]).astype(BF16)

    @pl.when(si == 0)
    def _():
        xb_buf[0:POOL_HALO, :] = jnp.zeros((POOL_HALO, B_WIDTH), F32)

    @pl.when(si > 0)
    def _():
        xb_buf[0:POOL_HALO, :] = xb_buf[tm:tm + POOL_HALO, :]

    xb_buf[POOL_HALO:POOL_HALO + tm, :] = z[:, 2 * A_WIDTH:]

    row = lax.broadcasted_iota(jnp.int32, (A_CHUNK, A_CHUNK), 0)
    col = lax.broadcasted_iota(jnp.int32, (A_CHUNK, A_CHUNK), 1)
    causal = col <= row
    for hd in range(A_HEADS):
        w_h = jnp.where(causal, ws[hd], 0.0).astype(BF16)
        bias = bs_t[:, hd:hd + 1]
        cs = slice(hd * A_HEAD_DIM, (hd + 1) * A_HEAD_DIM)
        for c in range(tm // A_CHUNK):
            rs = slice(c * A_CHUNK, (c + 1) * A_CHUNK)
            sv = jnp.dot(w_h, v[rs, cs], preferred_element_type=F32) + bias
            mix_buf[rs, cs] = (u[rs, cs] * sv).astype(BF16)

    t_pos = (si * tm + lax.broadcasted_iota(jnp.int32, (tm, 1), 0)).astype(F32)
    for g, win in enumerate(B_WINDOWS):
        cs = slice(g * B_GROUP_DIM, (g + 1) * B_GROUP_DIM)
        cur = xb_buf[POOL_HALO:POOL_HALO + tm, cs]
        acc = cur
        for j in range(1, win):
            acc = acc + xb_buf[POOL_HALO - j:POOL_HALO - j + tm, cs]
        cnt = jnp.minimum(t_pos + 1.0, float(win))
        pooled = acc / cnt - cur
        yb = _dot(pooled, w_pool[g]) * pool_scale[:, cs]
        mix_buf[:, A_WIDTH + g * B_GROUP_DIM:A_WIDTH + (g + 1) * B_GROUP_DIM] = yb.astype(BF16)

    y = jnp.dot(mix_buf[...], w_out[...], preferred_element_type=F32)
    o_ref[...] = x + _rms(y, post_g[...])


def _odd_step(t, n_seq_tiles, xc_ref, xo_ref, pre_g, w_main, w_gate, gate_bias, conv_w, h_gain,
              w_out, post_g, o_ref, cur, prev, h_new, h_old, c_st, n_st, m_st, y_buf):
    zq_c, zv_c, zo_c, zg_c = cur
    zq_p, zv_p, zo_p, zg_p = prev
    tm = xc_ref.shape[0]
    L = C_CHUNK
    seq_start = (jnp.maximum(t - 1, 0) % n_seq_tiles) == 0

    @pl.when(seq_start)
    def _():
        zq_p[0:CONV_HALO, :] = jnp.zeros((CONV_HALO, 2 * C_QK_WIDTH), F32)
        c_st[...] = jnp.zeros(c_st.shape, F32)
        n_st[...] = jnp.zeros(n_st.shape, F32)
        m_st[...] = jnp.zeros(m_st.shape, F32)

    @pl.when(jnp.logical_not(seq_start))
    def _():
        zq_p[0:CONV_HALO, :] = zq_c[tm:tm + CONV_HALO, :]

    hb = _rms(xc_ref[...], pre_g[...]).astype(BF16)

    def project_piece(j):
        cols = slice(j * PIECE, (j + 1) * PIECE)
        z = jnp.dot(hb, w_main[:, cols], preferred_element_type=F32)
        if cols.stop <= 2 * C_QK_WIDTH:
            zq_c[CONV_HALO:CONV_HALO + tm, cols] = z
        elif cols.stop <= 2 * C_QK_WIDTH + C_V_WIDTH:
            zv_c[:, cols.start - 2 * C_QK_WIDTH:cols.stop - 2 * C_QK_WIDTH] = z.astype(BF16)
        else:
            off = 2 * C_QK_WIDTH + C_V_WIDTH
            zo_c[:, cols.start - off:cols.stop - off] = z

    def project_gates():
        zg_c[...] = jnp.dot(hb, w_gate[...], preferred_element_type=F32) + gate_bias[...]

    def out_piece(j):
        cols = slice(j * PIECE, (j + 1) * PIECE)
        y_buf[:, cols] = jnp.dot(h_old[...], w_out[:, cols], preferred_element_type=F32)

    qk_pieces = 2 * C_QK_WIDTH // PIECE
    v_pieces = C_V_WIDTH // PIECE
    fill_head = [functools.partial(project_piece, j) for j in range(qk_pieces)]
    fill_intra = ([functools.partial(project_piece, qk_pieces + j) for j in range(v_pieces)]
                  + [project_gates])
    fill_rec = ([functools.partial(project_piece, qk_pieces + v_pieces + j) for j in range(v_pieces)]
                + [functools.partial(out_piece, j) for j in range(D_MODEL // PIECE)])

    def conv_silu(lo_col):
        cols = slice(lo_col, lo_col + C_QK_DIM)
        conv = zq_p[CONV_HALO:CONV_HALO + tm, cols] * conv_w[0:1, cols]
        for k in range(1, C_CONV):
            conv = conv + zq_p[CONV_HALO - k:CONV_HALO - k + tm, cols] * conv_w[k:k + 1, cols]
        return _silu(conv)

    row = lax.broadcasted_iota(jnp.int32, (L, L), 0)
    col = lax.broadcasted_iota(jnp.int32, (L, L), 1)
    causal = col <= row
    tri = jnp.where(causal, 1.0, 0.0).astype(BF16)

    n_chunks = tm // L
    rows_of = lambda c: slice(c * L, (c + 1) * L)
    v_of = lambda hd: slice(hd * C_V_DIM, (hd + 1) * C_V_DIM)
    pairs = [(c, hd) for c in range(n_chunks) for hd in range(C_HEADS)]

    ig_all = zg_p[:, :LANES]
    hi, mid, lo = _split3(_log_sigmoid(zg_p[:, LANES:]))
    gates = []
    for c in range(n_chunks):
        rs = rows_of(c)
        b_c = (jnp.dot(tri, hi[rs], preferred_element_type=F32)
               + jnp.dot(tri, mid[rs], preferred_element_type=F32)
               + jnp.dot(tri, lo[rs], preferred_element_type=F32))
        gates.append((ig_all[rs], b_c, ig_all[rs].T, b_c.T))
    q_of, k_of, qb_of, kb_of, qk_t = [], [], [], [], {}
    for hd in range(C_HEADS):
        q_of.append(conv_silu(hd * C_QK_DIM) * (C_QK_DIM ** -0.5))
        k_of.append(conv_silu(C_QK_WIDTH + hd * C_QK_DIM))
        qb_of.append(q_of[hd].astype(BF16))
        kb_of.append(k_of[hd].astype(BF16))
        if hd < len(fill_head):
            fill_head[hd]()
        for c in range(n_chunks):
            qk_t[c, hd] = lax.dot_general(qb_of[hd][rows_of(c)], kb_of[hd][rows_of(c)],
                                          (((1,), (1,)), ((), ())), preferred_element_type=F32)
    intra = {}
    for i, (c, hd) in enumerate(pairs):
        if i % 3 == 0 and i // 3 < len(fill_intra):
            fill_intra[i // 3]()
        rs = rows_of(c)
        ig_c, b_c, ig_t, b_t = gates[c]
        b_col = b_c[:, hd:hd + 1]
        ig_col = ig_c[:, hd:hd + 1]
        vh = zv_p[rs, v_of(hd)]
        dlog = jnp.where(causal, b_col - b_t[hd:hd + 1, :] + ig_t[hd:hd + 1, :], -jnp.inf)
        mt_i = jnp.max(dlog, axis=-1, keepdims=True)
        sc = qk_t[c, hd] * jnp.exp(dlog - mt_i)
        sv = jnp.dot(sc.astype(BF16), vh, preferred_element_type=F32)
        s_sum = jnp.sum(sc, axis=-1, keepdims=True)
        b_last = b_c[L - 1:L, hd:hd + 1]
        g_col = b_last - b_col + ig_col
        mg = jnp.max(g_col, axis=0, keepdims=True)
        ks = k_of[hd][rs] * jnp.exp(g_col - mg)
        kv = _dot(ks.T, vh)
        k_sum = jnp.sum(ks, axis=0, keepdims=True)
        intra[c, hd] = (b_col, b_last, mt_i, sv, s_sum, mg, kv, k_sum)

    state = [(c_st[hd], n_st[hd], m_st[hd][:, 0:1]) for hd in range(C_HEADS)]
    for i, (c, hd) in enumerate(pairs):
        if i % 2 == 0:
            fill_rec[i // 2]()
        rs = rows_of(c)
        c_prev, n_prev, m_prev = state[hd]
        b_col, b_last, mt_i, sv, s_sum, mg, kv, k_sum = intra[c, hd]
        a = b_col + m_prev
        mt = jnp.maximum(a, mt_i)
        wi = jnp.exp(a - mt)
        wn = jnp.exp(mt_i - mt)
        num = wi * _dot(qb_of[hd][rs], c_prev) + wn * sv
        den = wi * jnp.sum(q_of[hd][rs] * n_prev, axis=-1, keepdims=True) + wn * s_sum
        hh = num / jnp.maximum(jnp.abs(den), jnp.exp(-mt))
        m_new = jnp.maximum(b_last + m_prev, mg)
        wc = jnp.exp(b_last + m_prev - m_new)
        wk = jnp.exp(mg - m_new)
        state[hd] = (wc * c_prev + wk * kv, wc * n_prev + wk * k_sum, m_new)
        hn = _rms(hh, h_gain[:, v_of(hd)])
        h_new[rs, v_of(hd)] = (hn * _sigmoid(zo_p[rs, v_of(hd)])).astype(BF16)
    for hd in range(C_HEADS):
        c_st[hd], n_st[hd] = state[hd][0], state[hd][1]
        m_st[hd] = jnp.broadcast_to(state[hd][2], (1, LANES))

    o_ref[...] = xo_ref[...] + _rms(y_buf[...], post_g[...])


def _odd_kernel(xc_ref, xo_ref, pre_g, w_main, w_gate, gate_bias, conv_w, h_gain, w_out, post_g,
                o_ref, zq0, zq1, zv0, zv1, zo0, zo1, zg0, zg1, h0, h1, c_st, n_st, m_st, y_buf,
                *, n_seq_tiles):
    t = pl.program_id(0)
    slot0 = (zq0, zv0, zo0, zg0)
    slot1 = (zq1, zv1, zo1, zg1)

    @pl.when(t == 0)
    def _():
        h0[...] = jnp.zeros(h0.shape, BF16)
        zq1[...] = jnp.zeros(zq1.shape, F32)
        zv1[...] = jnp.zeros(zv1.shape, BF16)
        zo1[...] = jnp.zeros(zo1.shape, F32)
        zg1[...] = jnp.zeros(zg1.shape, F32)

    step = functools.partial(
        _odd_step, t, n_seq_tiles, xc_ref, xo_ref, pre_g, w_main, w_gate, gate_bias, conv_w,
        h_gain, w_out, post_g, o_ref)

    @pl.when(t % 2 == 0)
    def _():
        step(slot0, slot1, h1, h0, c_st, n_st, m_st, y_buf)

    @pl.when(t % 2 == 1)
    def _():
        step(slot1, slot0, h0, h1, c_st, n_st, m_st, y_buf)


def _ffn_kernel(x_ref, p_ref, pre_g, w_gu, w_down, post_g, w_pgate, w_pproj, ple_g, o_ref):
    x = x_ref[...]
    hn = _rms(x, pre_g[...]).astype(BF16)
    acc = jnp.zeros(x.shape, F32)
    for start, size in FF_CHUNKS:
        g = jnp.dot(hn, w_gu[:, start:start + size], preferred_element_type=F32)
        u = jnp.dot(hn, w_gu[:, D_FF + start:D_FF + start + size], preferred_element_type=F32)
        act = (_silu(g) * u).astype(BF16)
        acc = acc + jnp.dot(act, w_down[start:start + size, :], preferred_element_type=F32)
    x2 = x + _rms(acc, post_g[...])
    e = _sigmoid(_dot(x2, w_pgate[...])) * _dot(p_ref[...], w_pproj[...])
    o_ref[...] = x2 + _rms(e, ple_g[...])


def _layer_spec(stacked, layer):
    rest = stacked.shape[1:]
    zeros = (0,) * len(rest)
    return pl.BlockSpec((None,) + rest, lambda *_: (layer,) + zeros, pipeline_mode=pl.Buffered(1))


def _row_spec(n_seq_tiles, width):
    return pl.BlockSpec((TM, width), lambda b, s: (b * n_seq_tiles + s, 0))


_PARAMS = pltpu.CompilerParams(dimension_semantics=("arbitrary", "arbitrary"),
                               vmem_limit_bytes=VMEM_LIMIT)


def _even_call(x2d, batch, seq, layers, stacked):
    ns = seq // TM
    return pl.pallas_call(
        _even_kernel,
        out_shape=jax.ShapeDtypeStruct(x2d.shape, F32),
        grid=(batch, ns),
        in_specs=[_row_spec(ns, D_MODEL)] + [_layer_spec(a, l) for a, l in zip(stacked, layers)],
        out_specs=_row_spec(ns, D_MODEL),
        scratch_shapes=[pltpu.VMEM((POOL_HALO + TM, B_WIDTH), F32),
                        pltpu.VMEM((TM, EVEN_OUT), BF16)],
        compiler_params=_PARAMS,
        name="even_mixer",
    )(x2d, *stacked)


def _odd_call(x2d, batch, seq, layers, stacked):
    ns = seq // TM
    n_tiles = batch * ns
    cur_spec = pl.BlockSpec((TM, D_MODEL), lambda t: (jnp.minimum(t, n_tiles - 1), 0))
    old_spec = pl.BlockSpec((TM, D_MODEL), lambda t: (jnp.maximum(t - 2, 0), 0))
    slot = lambda width, dtype, halo=0: [pltpu.VMEM((halo + TM, width), dtype)] * 2
    return pl.pallas_call(
        functools.partial(_odd_kernel, n_seq_tiles=ns),
        out_shape=jax.ShapeDtypeStruct(x2d.shape, F32),
        grid=(n_tiles + 2,),
        in_specs=[cur_spec, old_spec] + [_layer_spec(a, l) for a, l in zip(stacked, layers)],
        out_specs=old_spec,
        scratch_shapes=(slot(2 * C_QK_WIDTH, F32, CONV_HALO) + slot(C_V_WIDTH, BF16)
                        + slot(C_V_WIDTH, F32) + slot(2 * LANES, F32) + slot(C_V_WIDTH, BF16)
                        + [pltpu.VMEM((C_HEADS, C_QK_DIM, C_V_DIM), F32),
                           pltpu.VMEM((C_HEADS, 1, C_QK_DIM), F32),
                           pltpu.VMEM((C_HEADS, 1, LANES), F32),
                           pltpu.VMEM((TM, D_MODEL), F32)]),
        compiler_params=pltpu.CompilerParams(dimension_semantics=("arbitrary",),
                                             vmem_limit_bytes=VMEM_LIMIT),
        name="odd_mixer",
    )(x2d, x2d, *stacked)


def _ffn_call(x2d, p3d, batch, seq, layer, stacked):
    ns = seq // TM
    p_spec = pl.BlockSpec((None, TM, PLE_DIM), lambda b, s: (layer, b * ns + s, 0))
    return pl.pallas_call(
        _ffn_kernel,
        out_shape=jax.ShapeDtypeStruct(x2d.shape, F32),
        grid=(batch, ns),
        in_specs=[_row_spec(ns, D_MODEL), p_spec] + [_layer_spec(a, layer) for a in stacked],
        out_specs=_row_spec(ns, D_MODEL),
        compiler_params=_PARAMS,
        name="ffn_ple",
    )(x2d, p3d, *stacked)


def kernel(x, p, mix_pre_gain, mix_post_gain, ffn_pre_gain, ffn_post_gain, ple_post_gain, even_w_in, even_a_v_gain, even_a_ws, even_a_bs, even_b_wpool, even_b_scale, even_w_out, odd_w_in, odd_conv_w, odd_b_i, odd_b_f, odd_h_gain, odd_w_out, ffn_w_gate_up, ffn_w_down, ple_proj, ple_gate):
    batch, seq, d = x.shape
    depth = p.shape[0]
    assert d == D_MODEL and seq % TM == 0 and TM % A_CHUNK == 0 and TM % C_CHUNK == 0
    rows = lambda a: a[:, None, :]

    mix_pre, mix_post = rows(mix_pre_gain), rows(mix_post_gain)
    even_stacked = (mix_pre, even_w_in.astype(BF16), rows(even_a_v_gain), even_a_ws,
                    jnp.swapaxes(even_a_bs, 1, 2), even_b_wpool.astype(BF16), rows(even_b_scale),
                    even_w_out.astype(BF16), mix_post)
    n_odd = odd_w_in.shape[0]
    w_gate = jnp.zeros((n_odd, d, 2 * LANES), F32)
    w_gate = w_gate.at[:, :, :C_HEADS].set(odd_w_in[:, :, ODD_MAIN:ODD_MAIN + C_HEADS])
    w_gate = w_gate.at[:, :, LANES:LANES + C_HEADS].set(odd_w_in[:, :, ODD_MAIN + C_HEADS:])
    gate_bias = jnp.zeros((n_odd, 1, 2 * LANES), F32)
    gate_bias = gate_bias.at[:, 0, :C_HEADS].set(odd_b_i)
    gate_bias = gate_bias.at[:, 0, LANES:LANES + C_HEADS].set(odd_b_f)
    odd_stacked = (mix_pre, odd_w_in[:, :, :ODD_MAIN].astype(BF16), w_gate.astype(BF16), gate_bias,
                   odd_conv_w, rows(odd_h_gain), odd_w_out.astype(BF16), mix_post)
    ffn_stacked = (rows(ffn_pre_gain), ffn_w_gate_up.astype(BF16), ffn_w_down.astype(BF16),
                   rows(ffn_post_gain), ple_gate.astype(BF16), ple_proj.astype(BF16),
                   rows(ple_post_gain))

    x2d = x.reshape(batch * seq, d)
    p3d = p.reshape(depth, batch * seq, PLE_DIM)
    for i in range(depth):
        j = i // 2
        if i % 2 == 0:
            x2d = _even_call(x2d, batch, seq, (i, j, j, j, j, j, j, j, i), even_stacked)
        else:
            x2d = _odd_call(x2d, batch, seq, (i, j, j, j, j, j, j, i), odd_stacked)
        x2d = _ffn_call(x2d, p3d, batch, seq, i, ffn_stacked)
    return x2d.reshape(batch, seq, d)
```

```python
import functools

import jax
import jax.numpy as jnp
from jax import lax
from jax.experimental import pallas as pl
from jax.experimental.pallas import tpu as pltpu

D_MODEL = 1024
A_HEADS = 4
A_HEAD_DIM = 128
A_WIDTH = A_HEADS * A_HEAD_DIM
A_CHUNK = 128
B_GROUPS = 4
B_GROUP_DIM = 128
B_WIDTH = B_GROUPS * B_GROUP_DIM
B_WINDOWS = (2, 4, 8, 16)
EVEN_IN = 2 * A_WIDTH + B_WIDTH
EVEN_OUT = A_WIDTH + B_WIDTH
C_HEADS = 4
C_QK_DIM = 128
C_V_DIM = 256
C_QK_WIDTH = C_HEADS * C_QK_DIM
C_V_WIDTH = C_HEADS * C_V_DIM
C_CHUNK = 128
C_CONV = 4
ODD_MAIN = 2 * C_QK_WIDTH + 2 * C_V_WIDTH
D_FF = 2816
PLE_DIM = 256
EPS = 1e-6

LANES = 128
SUBLANES = 8
POOL_HALO = 32
CONV_HALO = 8
TM = 512
TM_EVEN = 1024
FF_CHUNKS = ((0, 1024), (1024, 1024), (2048, 768))
PIECE = 256
VMEM_LIMIT = 56 * 1024 * 1024

F32 = jnp.float32
BF16 = jnp.bfloat16


def _rms(x, g):
    ms = jnp.mean(x * x, axis=-1, keepdims=True)
    return x * lax.rsqrt(ms + EPS) * g


def _dot(a, b):
    return jnp.dot(a.astype(BF16), b.astype(BF16), preferred_element_type=F32)


def _sigmoid(x):
    return 1.0 / (1.0 + jnp.exp(-x))


def _silu(x):
    return x * _sigmoid(x)


def _gelu_tanh(x):
    return 0.5 * x * (1.0 + jnp.tanh(0.7978845608028654 * (x + 0.044715 * (x * x * x))))


def _log_sigmoid(x):
    return jnp.minimum(x, 0.0) - jnp.log(1.0 + jnp.exp(-jnp.abs(x)))


def _split3(x):
    hi = x.astype(BF16)
    r = x - hi.astype(F32)
    mid = r.astype(BF16)
    lo = (r - mid.astype(F32)).astype(BF16)
    return hi, mid, lo


def _even_kernel(x_ref, pre_g, w_in, v_gain, ws, bs_t, w_pool, pool_scale, w_out, post_g,
                 o_ref, xb_buf, tmp_a, tmp_b, mix_buf):
    si = pl.program_id(1)
    tm = x_ref.shape[0]
    H = POOL_HALO
    x = x_ref[...]
    h = _rms(x, pre_g[...]).astype(BF16)

    @pl.when(si == 0)
    def _():
        xb_buf[0:H, :] = jnp.zeros((H, B_WIDTH), F32)

    @pl.when(si > 0)
    def _():
        xb_buf[0:H, :] = xb_buf[tm:tm + H, :]

    xb_buf[H:H + tm, :] = jnp.dot(h, w_in[:, 2 * A_WIDTH:], preferred_element_type=F32)
    zv = jnp.dot(h, w_in[:, A_WIDTH:2 * A_WIDTH], preferred_element_type=F32)
    zu = jnp.dot(h, w_in[:, :A_WIDTH], preferred_element_type=F32)

    t_pos = (si * tm + lax.broadcasted_iota(jnp.int32, (tm, 1), 0)).astype(F32)
    for g, win in enumerate(B_WINDOWS):
        cs = slice(g * B_GROUP_DIM, (g + 1) * B_GROUP_DIM)
        levels = win.bit_length() - 1
        src, src_cols, dst = xb_buf, cs, tmp_a
        acc = None
        for j in range(1, levels + 1):
            step = 1 << (j - 1)
            lo = H - SUBLANES * (levels - j)
            if step % SUBLANES == 0:
                acc = acc[step:, :] + acc[:-step, :]
            else:
                if j > 1:
                    src[lo - SUBLANES:H + tm, src_cols] = acc
                acc = src[lo:H + tm, src_cols] + src[lo - step:H + tm - step, src_cols]
            src, src_cols, dst = dst, slice(None), (tmp_b if dst is tmp_a else tmp_a)
        cur = xb_buf[H:H + tm, cs]
        cnt = jnp.minimum(t_pos + 1.0, float(win))
        pooled = acc / cnt - cur
        yb = _dot(pooled, w_pool[g]) * pool_scale[:, cs]
        mix_buf[:, A_WIDTH + g * B_GROUP_DIM:A_WIDTH + (g + 1) * B_GROUP_DIM] = yb.astype(BF16)

    v = _rms(_gelu_tanh(zv), v_gain[...]).astype(BF16)
    u = _gelu_tanh(zu)
    row = lax.broadcasted_iota(jnp.int32, (A_CHUNK, A_CHUNK), 0)
    col = lax.broadcasted_iota(jnp.int32, (A_CHUNK, A_CHUNK), 1)
    causal = col <= row
    for hd in range(A_HEADS):
        w_h = jnp.where(causal, ws[hd], 0.0).astype(BF16)
        bias = bs_t[:, hd:hd + 1]
        cs = slice(hd * A_HEAD_DIM, (hd + 1) * A_HEAD_DIM)
        for c in range(tm // A_CHUNK):
            rs = slice(c * A_CHUNK, (c + 1) * A_CHUNK)
            sv = jnp.dot(w_h, v[rs, cs], preferred_element_type=F32) + bias
            mix_buf[rs, cs] = (u[rs, cs] * sv).astype(BF16)

    y = jnp.dot(mix_buf[...], w_out[...], preferred_element_type=F32)
    o_ref[...] = x + _rms(y, post_g[...])


def _odd_step(t, n_seq_tiles, xc_ref, xo_ref, pre_g, w_main, w_gate, gate_bias, conv_w, h_gain,
              w_out, post_g, o_ref, cur, prev, h_new, h_old, c_st, n_st, m_st, y_buf):
    zq_c, zv_c, zo_c, zg_c = cur
    zq_p, zv_p, zo_p, zg_p = prev
    tm = xc_ref.shape[0]
    L = C_CHUNK
    seq_start = (jnp.maximum(t - 1, 0) % n_seq_tiles) == 0

    @pl.when(seq_start)
    def _():
        zq_p[0:CONV_HALO, :] = jnp.zeros((CONV_HALO, 2 * C_QK_WIDTH), F32)
        c_st[...] = jnp.zeros(c_st.shape, F32)
        n_st[...] = jnp.zeros(n_st.shape, F32)
        m_st[...] = jnp.zeros(m_st.shape, F32)

    @pl.when(jnp.logical_not(seq_start))
    def _():
        zq_p[0:CONV_HALO, :] = zq_c[tm:tm + CONV_HALO, :]

    hb = _rms(xc_ref[...], pre_g[...]).astype(BF16)

    def project_piece(j):
        cols = slice(j * PIECE, (j + 1) * PIECE)
        z = jnp.dot(hb, w_main[:, cols], preferred_element_type=F32)
        if cols.stop <= 2 * C_QK_WIDTH:
            zq_c[CONV_HALO:CONV_HALO + tm, cols] = z
        elif cols.stop <= 2 * C_QK_WIDTH + C_V_WIDTH:
            zv_c[:, cols.start - 2 * C_QK_WIDTH:cols.stop - 2 * C_QK_WIDTH] = z.astype(BF16)
        else:
            off = 2 * C_QK_WIDTH + C_V_WIDTH
            zo_c[:, cols.start - off:cols.stop - off] = z

    def project_gates():
        zg_c[...] = jnp.dot(hb, w_gate[...], preferred_element_type=F32) + gate_bias[...]

    def out_piece(j):
        cols = slice(j * PIECE, (j + 1) * PIECE)
        y_buf[:, cols] = jnp.dot(h_old[...], w_out[:, cols], preferred_element_type=F32)

    qk_pieces = 2 * C_QK_WIDTH // PIECE
    v_pieces = C_V_WIDTH // PIECE
    fill_head = [functools.partial(project_piece, j) for j in range(qk_pieces)]
    fill_intra = ([functools.partial(project_piece, qk_pieces + j) for j in range(v_pieces)]
                  + [project_gates])
    fill_rec = ([functools.partial(project_piece, qk_pieces + v_pieces + j) for j in range(v_pieces)]
                + [functools.partial(out_piece, j) for j in range(D_MODEL // PIECE)])

    def conv_silu(lo_col):
        cols = slice(lo_col, lo_col + C_QK_DIM)
        conv = zq_p[CONV_HALO:CONV_HALO + tm, cols] * conv_w[0:1, cols]
        for k in range(1, C_CONV):
            conv = conv + zq_p[CONV_HALO - k:CONV_HALO - k + tm, cols] * conv_w[k:k + 1, cols]
        return _silu(conv)

    row = lax.broadcasted_iota(jnp.int32, (L, L), 0)
    col = lax.broadcasted_iota(jnp.int32, (L, L), 1)
    causal = col <= row
    tri = jnp.where(causal, 1.0, 0.0).astype(BF16)

    n_chunks = tm // L
    rows_of = lambda c: slice(c * L, (c + 1) * L)
    v_of = lambda hd: slice(hd * C_V_DIM, (hd + 1) * C_V_DIM)
    pairs = [(c, hd) for c in range(n_chunks) for hd in range(C_HEADS)]

    ig_all = zg_p[:, :LANES]
    hi, mid, lo = _split3(_log_sigmoid(zg_p[:, LANES:]))
    gates = []
    for c in range(n_chunks):
        rs = rows_of(c)
        b_c = (jnp.dot(tri, hi[rs], preferred_element_type=F32)
               + jnp.dot(tri, mid[rs], preferred_element_type=F32)
               + jnp.dot(tri, lo[rs], preferred_element_type=F32))
        gates.append((ig_all[rs], b_c, ig_all[rs].T, b_c.T))
    q_of, k_of, qb_of, kb_of, qk_t = [], [], [], [], {}
    for hd in range(C_HEADS):
        q_of.append(conv_silu(hd * C_QK_DIM) * (C_QK_DIM ** -0.5))
        k_of.append(conv_silu(C_QK_WIDTH + hd * C_QK_DIM))
        qb_of.append(q_of[hd].astype(BF16))
        kb_of.append(k_of[hd].astype(BF16))
        if hd < len(fill_head):
            fill_head[hd]()
        for c in range(n_chunks):
            qk_t[c, hd] = lax.dot_general(qb_of[hd][rows_of(c)], kb_of[hd][rows_of(c)],
                                          (((1,), (1,)), ((), ())), preferred_element_type=F32)
    intra = {}
    for i, (c, hd) in enumerate(pairs):
        if i % 3 == 0 and i // 3 < len(fill_intra):
            fill_intra[i // 3]()
        rs = rows_of(c)
        ig_c, b_c, ig_t, b_t = gates[c]
        b_col = b_c[:, hd:hd + 1]
        ig_col = ig_c[:, hd:hd + 1]
        vh = zv_p[rs, v_of(hd)]
        dlog = jnp.where(causal, b_col - b_t[hd:hd + 1, :] + ig_t[hd:hd + 1, :], -jnp.inf)
        mt_i = jnp.max(dlog, axis=-1, keepdims=True)
        sc = qk_t[c, hd] * jnp.exp(dlog - mt_i)
        sv = jnp.dot(sc.astype(BF16), vh, preferred_element_type=F32)
        s_sum = jnp.sum(sc, axis=-1, keepdims=True)
        b_last = b_c[L - 1:L, hd:hd + 1]
        g_col = b_last - b_col + ig_col
        mg = jnp.max(g_col, axis=0, keepdims=True)
        ks = k_of[hd][rs] * jnp.exp(g_col - mg)
        kv = _dot(ks.T, vh)
        k_sum = jnp.sum(ks, axis=0, keepdims=True)
        intra[c, hd] = (b_col, b_last, mt_i, sv, s_sum, mg, kv, k_sum)

    state = [(c_st[hd], n_st[hd], m_st[hd][:, 0:1]) for hd in range(C_HEADS)]
    for i, (c, hd) in enumerate(pairs):
        if i % 2 == 0:
            fill_rec[i // 2]()
        rs = rows_of(c)
        c_prev, n_prev, m_prev = state[hd]
        b_col, b_last, mt_i, sv, s_sum, mg, kv, k_sum = intra[c, hd]
        a = b_col + m_prev
        mt = jnp.maximum(a, mt_i)
        wi = jnp.exp(a - mt)
        wn = jnp.exp(mt_i - mt)
        num = wi * _dot(qb_of[hd][rs], c_prev) + wn * sv
        den = wi * jnp.sum(q_of[hd][rs] * n_prev, axis=-1, keepdims=True) + wn * s_sum
        hh = num / jnp.maximum(jnp.abs(den), jnp.exp(-mt))
        m_new = jnp.maximum(b_last + m_prev, mg)
        wc = jnp.exp(b_last + m_prev - m_new)
        wk = jnp.exp(mg - m_new)
        state[hd] = (wc * c_prev + wk * kv, wc * n_prev + wk * k_sum, m_new)
        hn = _rms(hh, h_gain[:, v_of(hd)])
        h_new[rs, v_of(hd)] = (hn * _sigmoid(zo_p[rs, v_of(hd)])).astype(BF16)
    for hd in range(C_HEADS):
        c_st[hd], n_st[hd] = state[hd][0], state[hd][1]
        m_st[hd] = jnp.broadcast_to(state[hd][2], (1, LANES))

    o_ref[...] = xo_ref[...] + _rms(y_buf[...], post_g[...])


def _odd_kernel(xc_ref, xo_ref, pre_g, w_main, w_gate, gate_bias, conv_w, h_gain, w_out, post_g,
                o_ref, zq0, zq1, zv0, zv1, zo0, zo1, zg0, zg1, h0, h1, c_st, n_st, m_st, y_buf,
                *, n_seq_tiles):
    t = pl.program_id(0)
    slot0 = (zq0, zv0, zo0, zg0)
    slot1 = (zq1, zv1, zo1, zg1)

    @pl.when(t == 0)
    def _():
        h0[...] = jnp.zeros(h0.shape, BF16)
        zq1[...] = jnp.zeros(zq1.shape, F32)
        zv1[...] = jnp.zeros(zv1.shape, BF16)
        zo1[...] = jnp.zeros(zo1.shape, F32)
        zg1[...] = jnp.zeros(zg1.shape, F32)

    step = functools.partial(
        _odd_step, t, n_seq_tiles, xc_ref, xo_ref, pre_g, w_main, w_gate, gate_bias, conv_w,
        h_gain, w_out, post_g, o_ref)

    @pl.when(t % 2 == 0)
    def _():
        step(slot0, slot1, h1, h0, c_st, n_st, m_st, y_buf)

    @pl.when(t % 2 == 1)
    def _():
        step(slot1, slot0, h0, h1, c_st, n_st, m_st, y_buf)


def _ffn_kernel(x_ref, p_ref, pre_g, w_gu, w_down, post_g, w_pgate, w_pproj, ple_g, o_ref):
    x = x_ref[...]
    hn = _rms(x, pre_g[...]).astype(BF16)
    def gate_up(start, size):
        g = jnp.dot(hn, w_gu[:, start:start + size], preferred_element_type=F32)
        u = jnp.dot(hn, w_gu[:, D_FF + start:D_FF + start + size], preferred_element_type=F32)
        return g, u

    acc = None
    pending = gate_up(*FF_CHUNKS[0])
    for j, (start, size) in enumerate(FF_CHUNKS):
        g, u = pending
        if j + 1 < len(FF_CHUNKS):
            pending = gate_up(*FF_CHUNKS[j + 1])
        act = (_silu(g) * u).astype(BF16)
        d = jnp.dot(act, w_down[start:start + size, :], preferred_element_type=F32)
        acc = d if acc is None else acc + d
    emb = _dot(p_ref[...], w_pproj[...])
    x2 = x + _rms(acc, post_g[...])
    e = _sigmoid(_dot(x2, w_pgate[...])) * emb
    o_ref[...] = x2 + _rms(e, ple_g[...])


def _layer_spec(stacked, layer):
    rest = stacked.shape[1:]
    zeros = (0,) * len(rest)
    return pl.BlockSpec((None,) + rest, lambda *_: (layer,) + zeros, pipeline_mode=pl.Buffered(1))


def _row_spec(n_seq_tiles, width, tm=TM):
    return pl.BlockSpec((tm, width), lambda b, s: (b * n_seq_tiles + s, 0))


_PARAMS = pltpu.CompilerParams(dimension_semantics=("arbitrary", "arbitrary"),
                               vmem_limit_bytes=VMEM_LIMIT)


def _even_call(x2d, batch, seq, layers, stacked):
    ns = seq // TM_EVEN
    return pl.pallas_call(
        _even_kernel,
        out_shape=jax.ShapeDtypeStruct(x2d.shape, F32),
        grid=(batch, ns),
        in_specs=[_row_spec(ns, D_MODEL, TM_EVEN)]
                 + [_layer_spec(a, l) for a, l in zip(stacked, layers)],
        out_specs=_row_spec(ns, D_MODEL, TM_EVEN),
        scratch_shapes=[pltpu.VMEM((POOL_HALO + TM_EVEN, B_WIDTH), F32),
                        pltpu.VMEM((POOL_HALO + TM_EVEN, B_GROUP_DIM), F32),
                        pltpu.VMEM((POOL_HALO + TM_EVEN, B_GROUP_DIM), F32),
                        pltpu.VMEM((TM_EVEN, EVEN_OUT), BF16)],
        compiler_params=_PARAMS,
        name="even_mixer",
    )(x2d, *stacked)


def _odd_call(x2d, batch, seq, layers, stacked):
    ns = seq // TM
    n_tiles = batch * ns
    cur_spec = pl.BlockSpec((TM, D_MODEL), lambda t: (jnp.minimum(t, n_tiles - 1), 0))
    old_spec = pl.BlockSpec((TM, D_MODEL), lambda t: (jnp.maximum(t - 2, 0), 0))
    slot = lambda width, dtype, halo=0: [pltpu.VMEM((halo + TM, width), dtype)] * 2
    return pl.pallas_call(
        functools.partial(_odd_kernel, n_seq_tiles=ns),
        out_shape=jax.ShapeDtypeStruct(x2d.shape, F32),
        grid=(n_tiles + 2,),
        in_specs=[cur_spec, old_spec] + [_layer_spec(a, l) for a, l in zip(stacked, layers)],
        out_specs=old_spec,
        scratch_shapes=(slot(2 * C_QK_WIDTH, F32, CONV_HALO) + slot(C_V_WIDTH, BF16)
                        + slot(C_V_WIDTH, F32) + slot(2 * LANES, F32) + slot(C_V_WIDTH, BF16)
                        + [pltpu.VMEM((C_HEADS, C_QK_DIM, C_V_DIM), F32),
                           pltpu.VMEM((C_HEADS, 1, C_QK_DIM), F32),
                           pltpu.VMEM((C_HEADS, 1, LANES), F32),
                           pltpu.VMEM((TM, D_MODEL), F32)]),
        compiler_params=pltpu.CompilerParams(dimension_semantics=("arbitrary",),
                                             vmem_limit_bytes=VMEM_LIMIT),
        name="odd_mixer",
    )(x2d, x2d, *stacked)


def _ffn_call(x2d, p3d, batch, seq, layer, stacked):
    ns = seq // TM
    p_spec = pl.BlockSpec((None, TM, PLE_DIM), lambda b, s: (layer, b * ns + s, 0))
    return pl.pallas_call(
        _ffn_kernel,
        out_shape=jax.ShapeDtypeStruct(x2d.shape, F32),
        grid=(batch, ns),
        in_specs=[_row_spec(ns, D_MODEL), p_spec] + [_layer_spec(a, layer) for a in stacked],
        out_specs=_row_spec(ns, D_MODEL),
        compiler_params=_PARAMS,
        name="ffn_ple",
    )(x2d, p3d, *stacked)


def kernel(x, p, mix_pre_gain, mix_post_gain, ffn_pre_gain, ffn_post_gain, ple_post_gain, even_w_in, even_a_v_gain, even_a_ws, even_a_bs, even_b_wpool, even_b_scale, even_w_out, odd_w_in, odd_conv_w, odd_b_i, odd_b_f, odd_h_gain, odd_w_out, ffn_w_gate_up, ffn_w_down, ple_proj, ple_gate):
    batch, seq, d = x.shape
    depth = p.shape[0]
    assert d == D_MODEL and seq % TM == 0 and TM % C_CHUNK == 0
    assert seq % TM_EVEN == 0 and TM_EVEN % A_CHUNK == 0
    rows = lambda a: a[:, None, :]

    mix_pre, mix_post = rows(mix_pre_gain), rows(mix_post_gain)
    even_stacked = (mix_pre, even_w_in.astype(BF16), rows(even_a_v_gain), even_a_ws,
                    jnp.swapaxes(even_a_bs, 1, 2), even_b_wpool.astype(BF16), rows(even_b_scale),
                    even_w_out.astype(BF16), mix_post)
    n_odd = odd_w_in.shape[0]
    w_gate = jnp.zeros((n_odd, d, 2 * LANES), F32)
    w_gate = w_gate.at[:, :, :C_HEADS].set(odd_w_in[:, :, ODD_MAIN:ODD_MAIN + C_HEADS])
    w_gate = w_gate.at[:, :, LANES:LANES + C_HEADS].set(odd_w_in[:, :, ODD_MAIN + C_HEADS:])
    gate_bias = jnp.zeros((n_odd, 1, 2 * LANES), F32)
    gate_bias = gate_bias.at[:, 0, :C_HEADS].set(odd_b_i)
    gate_bias = gate_bias.at[:, 0, LANES:LANES + C_HEADS].set(odd_b_f)
    odd_stacked = (mix_pre, odd_w_in[:, :, :ODD_MAIN].astype(BF16), w_gate.astype(BF16), gate_bias,
                   odd_conv_w, rows(odd_h_gain), odd_w_out.astype(BF16), mix_post)
    ffn_stacked = (rows(ffn_pre_gain), ffn_w_gate_up.astype(BF16), ffn_w_down.astype(BF16),
                   rows(ffn_post_gain), ple_gate.astype(BF16), ple_proj.astype(BF16),
                   rows(ple_post_gain))

    x2d = x.reshape(batch * seq, d)
    p3d = p.reshape(depth, batch * seq, PLE_DIM)
    for i in range(depth):
        j = i // 2
        if i % 2 == 0:
            x2d = _even_call(x2d, batch, seq, (i, j, j, j, j, j, j, j, i), even_stacked)
        else:
            x2d = _odd_call(x2d, batch, seq, (i, j, j, j, j, j, j, i), odd_stacked)
        x2d = _ffn_call(x2d, p3d, batch, seq, i, ffn_stacked)
    return x2d.reshape(batch, seq, d)
```

```python
import functools

import jax
import jax.numpy as jnp
from jax import lax
from jax.experimental import pallas as pl
from jax.experimental.pallas import tpu as pltpu

D_MODEL = 1024
A_HEADS = 4
A_HEAD_DIM = 128
A_WIDTH = A_HEADS * A_HEAD_DIM
A_CHUNK = 128
B_GROUPS = 4
B_GROUP_DIM = 128
B_WIDTH = B_GROUPS * B_GROUP_DIM
B_WINDOWS = (2, 4, 8, 16)
EVEN_IN = 2 * A_WIDTH + B_WIDTH
EVEN_OUT = A_WIDTH + B_WIDTH
C_HEADS = 4
C_QK_DIM = 128
C_V_DIM = 256
C_QK_WIDTH = C_HEADS * C_QK_DIM
C_V_WIDTH = C_HEADS * C_V_DIM
C_CHUNK = 128
C_CONV = 4
ODD_MAIN = 2 * C_QK_WIDTH + 2 * C_V_WIDTH
D_FF = 2816
PLE_DIM = 256
EPS = 1e-6

LANES = 128
SUBLANES = 8
POOL_HALO = 32
CONV_HALO = 8
TM = 512
TM_EVEN = 1024
EVEN_SPLIT = 2
FF_CHUNKS = ((0, 1024), (1024, 1024), (2048, 768))
PIECE = 256
VMEM_LIMIT = 56 * 1024 * 1024

F32 = jnp.float32
BF16 = jnp.bfloat16


def _rms(x, g):
    ms = jnp.mean(x * x, axis=-1, keepdims=True)
    return x * lax.rsqrt(ms + EPS) * g


def _dot(a, b):
    return jnp.dot(a.astype(BF16), b.astype(BF16), preferred_element_type=F32)


def _sigmoid(x):
    return 1.0 / (1.0 + jnp.exp(-x))


def _silu(x):
    return x * _sigmoid(x)


def _gelu_tanh(x):
    return 0.5 * x * (1.0 + jnp.tanh(0.7978845608028654 * (x + 0.044715 * (x * x * x))))


def _log_sigmoid(x):
    return jnp.minimum(x, 0.0) - jnp.log(1.0 + jnp.exp(-jnp.abs(x)))


def _split3(x):
    hi = x.astype(BF16)
    r = x - hi.astype(F32)
    mid = r.astype(BF16)
    lo = (r - mid.astype(F32)).astype(BF16)
    return hi, mid, lo


def _even_kernel(x_ref, pre_g, w_in, v_gain, ws, bs_t, w_pool, pool_scale, w_out, post_g,
                 o_ref, xb_buf, tmp_a, tmp_b, mix_buf):
    si = pl.program_id(1)
    tm = x_ref.shape[0]
    H = POOL_HALO
    parts = [slice(k * (tm // EVEN_SPLIT), (k + 1) * (tm // EVEN_SPLIT)) for k in range(EVEN_SPLIT)]

    @pl.when(si == 0)
    def _():
        xb_buf[0:H, :] = jnp.zeros((H, B_WIDTH), F32)

    @pl.when(si > 0)
    def _():
        xb_buf[0:H, :] = xb_buf[tm:tm + H, :]

    zv, zu = [], []
    for rs in parts:
        h = _rms(x_ref[rs, :], pre_g[...]).astype(BF16)
        xb_buf[H + rs.start:H + rs.stop, :] = jnp.dot(h, w_in[:, 2 * A_WIDTH:],
                                                     preferred_element_type=F32)
        zv.append(jnp.dot(h, w_in[:, A_WIDTH:2 * A_WIDTH], preferred_element_type=F32))
        zu.append(jnp.dot(h, w_in[:, :A_WIDTH], preferred_element_type=F32))
    zv = jnp.concatenate(zv, axis=0)
    zu = jnp.concatenate(zu, axis=0)

    t_pos = (si * tm + lax.broadcasted_iota(jnp.int32, (tm, 1), 0)).astype(F32)
    for g, win in enumerate(B_WINDOWS):
        cs = slice(g * B_GROUP_DIM, (g + 1) * B_GROUP_DIM)
        levels = win.bit_length() - 1
        src, src_cols, dst = xb_buf, cs, tmp_a
        acc = None
        for j in range(1, levels + 1):
            step = 1 << (j - 1)
            lo = H - SUBLANES * (levels - j)
            if step % SUBLANES == 0:
                acc = acc[step:, :] + acc[:-step, :]
            else:
                if j > 1:
                    src[lo - SUBLANES:H + tm, src_cols] = acc
                acc = src[lo:H + tm, src_cols] + src[lo - step:H + tm - step, src_cols]
            src, src_cols, dst = dst, slice(None), (tmp_b if dst is tmp_a else tmp_a)
        cur = xb_buf[H:H + tm, cs]
        cnt = jnp.minimum(t_pos + 1.0, float(win))
        pooled = acc / cnt - cur
        yb = _dot(pooled, w_pool[g]) * pool_scale[:, cs]
        mix_buf[:, A_WIDTH + g * B_GROUP_DIM:A_WIDTH + (g + 1) * B_GROUP_DIM] = yb.astype(BF16)

    v = _rms(_gelu_tanh(zv), v_gain[...]).astype(BF16)
    u = _gelu_tanh(zu)
    row = lax.broadcasted_iota(jnp.int32, (A_CHUNK, A_CHUNK), 0)
    col = lax.broadcasted_iota(jnp.int32, (A_CHUNK, A_CHUNK), 1)
    causal = col <= row
    for hd in range(A_HEADS):
        w_h = jnp.where(causal, ws[hd], 0.0).astype(BF16)
        bias = bs_t[:, hd:hd + 1]
        cs = slice(hd * A_HEAD_DIM, (hd + 1) * A_HEAD_DIM)
        for c in range(tm // A_CHUNK):
            rs = slice(c * A_CHUNK, (c + 1) * A_CHUNK)
            sv = jnp.dot(w_h, v[rs, cs], preferred_element_type=F32) + bias
            mix_buf[rs, cs] = (u[rs, cs] * sv).astype(BF16)

    for rs in parts:
        y = jnp.dot(mix_buf[rs, :], w_out[...], preferred_element_type=F32)
        o_ref[rs, :] = x_ref[rs, :] + _rms(y, post_g[...])


def _odd_step(t, n_seq_tiles, xc_ref, xo_ref, pre_g, w_main, w_gate, gate_bias, conv_w, h_gain,
              w_out, post_g, o_ref, cur, prev, h_new, h_old, c_st, n_st, m_st, y_buf):
    zq_c, zv_c, zo_c, zg_c = cur
    zq_p, zv_p, zo_p, zg_p = prev
    tm = xc_ref.shape[0]
    L = C_CHUNK
    seq_start = (jnp.maximum(t - 1, 0) % n_seq_tiles) == 0

    @pl.when(seq_start)
    def _():
        zq_p[0:CONV_HALO, :] = jnp.zeros((CONV_HALO, 2 * C_QK_WIDTH), F32)
        c_st[...] = jnp.zeros(c_st.shape, F32)
        n_st[...] = jnp.zeros(n_st.shape, F32)
        m_st[...] = jnp.zeros(m_st.shape, F32)

    @pl.when(jnp.logical_not(seq_start))
    def _():
        zq_p[0:CONV_HALO, :] = zq_c[tm:tm + CONV_HALO, :]

    hb = _rms(xc_ref[...], pre_g[...]).astype(BF16)

    def project_piece(j):
        cols = slice(j * PIECE, (j + 1) * PIECE)
        z = jnp.dot(hb, w_main[:, cols], preferred_element_type=F32)
        if cols.stop <= 2 * C_QK_WIDTH:
            zq_c[CONV_HALO:CONV_HALO + tm, cols] = z
        elif cols.stop <= 2 * C_QK_WIDTH + C_V_WIDTH:
            zv_c[:, cols.start - 2 * C_QK_WIDTH:cols.stop - 2 * C_QK_WIDTH] = z.astype(BF16)
        else:
            off = 2 * C_QK_WIDTH + C_V_WIDTH
            zo_c[:, cols.start - off:cols.stop - off] = z

    def project_gates():
        zg_c[...] = jnp.dot(hb, w_gate[...], preferred_element_type=F32) + gate_bias[...]

    def out_piece(j):
        cols = slice(j * PIECE, (j + 1) * PIECE)
        y_buf[:, cols] = jnp.dot(h_old[...], w_out[:, cols], preferred_element_type=F32)

    qk_pieces = 2 * C_QK_WIDTH // PIECE
    v_pieces = C_V_WIDTH // PIECE
    fill_head = [functools.partial(project_piece, j) for j in range(qk_pieces)]
    fill_intra = ([functools.partial(project_piece, qk_pieces + j) for j in range(v_pieces)]
                  + [project_gates])
    fill_rec = ([functools.partial(project_piece, qk_pieces + v_pieces + j) for j in range(v_pieces)]
                + [functools.partial(out_piece, j) for j in range(D_MODEL // PIECE)])

    def conv_silu(lo_col):
        cols = slice(lo_col, lo_col + C_QK_DIM)
        conv = zq_p[CONV_HALO:CONV_HALO + tm, cols] * conv_w[0:1, cols]
        for k in range(1, C_CONV):
            conv = conv + zq_p[CONV_HALO - k:CONV_HALO - k + tm, cols] * conv_w[k:k + 1, cols]
        return _silu(conv)

    row = lax.broadcasted_iota(jnp.int32, (L, L), 0)
    col = lax.broadcasted_iota(jnp.int32, (L, L), 1)
    causal = col <= row
    tri = jnp.where(causal, 1.0, 0.0).astype(BF16)

    n_chunks = tm // L
    rows_of = lambda c: slice(c * L, (c + 1) * L)
    v_of = lambda hd: slice(hd * C_V_DIM, (hd + 1) * C_V_DIM)
    pairs = [(c, hd) for c in range(n_chunks) for hd in range(C_HEADS)]

    ig_all = zg_p[:, :LANES]
    hi, mid, lo = _split3(_log_sigmoid(zg_p[:, LANES:]))
    gates = []
    for c in range(n_chunks):
        rs = rows_of(c)
        b_c = (jnp.dot(tri, hi[rs], preferred_element_type=F32)
               + jnp.dot(tri, mid[rs], preferred_element_type=F32)
               + jnp.dot(tri, lo[rs], preferred_element_type=F32))
        gates.append((ig_all[rs], b_c, ig_all[rs].T, b_c.T))
    q_of, k_of, qb_of, kb_of, qk_t = [], [], [], [], {}
    for hd in range(C_HEADS):
        q_of.append(conv_silu(hd * C_QK_DIM) * (C_QK_DIM ** -0.5))
        k_of.append(conv_silu(C_QK_WIDTH + hd * C_QK_DIM))
        qb_of.append(q_of[hd].astype(BF16))
        kb_of.append(k_of[hd].astype(BF16))
        if hd < len(fill_head):
            fill_head[hd]()
        for c in range(n_chunks):
            qk_t[c, hd] = lax.dot_general(qb_of[hd][rows_of(c)], kb_of[hd][rows_of(c)],
                                          (((1,), (1,)), ((), ())), preferred_element_type=F32)
    intra = {}
    for i, (c, hd) in enumerate(pairs):
        if i % 3 == 0 and i // 3 < len(fill_intra):
            fill_intra[i // 3]()
        rs = rows_of(c)
        ig_c, b_c, ig_t, b_t = gates[c]
        b_col = b_c[:, hd:hd + 1]
        ig_col = ig_c[:, hd:hd + 1]
        vh = zv_p[rs, v_of(hd)]
        dlog = jnp.where(causal, b_col - b_t[hd:hd + 1, :] + ig_t[hd:hd + 1, :], -jnp.inf)
        mt_i = jnp.max(dlog, axis=-1, keepdims=True)
        sc = qk_t[c, hd] * jnp.exp(dlog - mt_i)
        sv = jnp.dot(sc.astype(BF16), vh, preferred_element_type=F32)
        s_sum = jnp.sum(sc, axis=-1, keepdims=True)
        b_last = b_c[L - 1:L, hd:hd + 1]
        g_col = b_last - b_col + ig_col
        mg = jnp.max(g_col, axis=0, keepdims=True)
        ks = k_of[hd][rs] * jnp.exp(g_col - mg)
        kv = _dot(ks.T, vh)
        k_sum = jnp.sum(ks, axis=0, keepdims=True)
        intra[c, hd] = (b_col, b_last, mt_i, sv, s_sum, mg, kv, k_sum)

    state = [(c_st[hd], n_st[hd], m_st[hd][:, 0:1]) for hd in range(C_HEADS)]
    for i, (c, hd) in enumerate(pairs):
        if i % 2 == 0:
            fill_rec[i // 2]()
        rs = rows_of(c)
        c_prev, n_prev, m_prev = state[hd]
        b_col, b_last, mt_i, sv, s_sum, mg, kv, k_sum = intra[c, hd]
        a = b_col + m_prev
        mt = jnp.maximum(a, mt_i)
        wi = jnp.exp(a - mt)
        wn = jnp.exp(mt_i - mt)
        num = wi * _dot(qb_of[hd][rs], c_prev) + wn * sv
        den = wi * jnp.sum(q_of[hd][rs] * n_prev, axis=-1, keepdims=True) + wn * s_sum
        hh = num / jnp.maximum(jnp.abs(den), jnp.exp(-mt))
        m_new = jnp.maximum(b_last + m_prev, mg)
        wc = jnp.exp(b_last + m_prev - m_new)
        wk = jnp.exp(mg - m_new)
        state[hd] = (wc * c_prev + wk * kv, wc * n_prev + wk * k_sum, m_new)
        hn = _rms(hh, h_gain[:, v_of(hd)])
        h_new[rs, v_of(hd)] = (hn * _sigmoid(zo_p[rs, v_of(hd)])).astype(BF16)
    for hd in range(C_HEADS):
        c_st[hd], n_st[hd] = state[hd][0], state[hd][1]
        m_st[hd] = jnp.broadcast_to(state[hd][2], (1, LANES))

    o_ref[...] = xo_ref[...] + _rms(y_buf[...], post_g[...])


def _odd_kernel(xc_ref, xo_ref, pre_g, w_main, w_gate, gate_bias, conv_w, h_gain, w_out, post_g,
                o_ref, zq0, zq1, zv0, zv1, zo0, zo1, zg0, zg1, h0, h1, c_st, n_st, m_st, y_buf,
                *, n_seq_tiles):
    t = pl.program_id(0)
    slot0 = (zq0, zv0, zo0, zg0)
    slot1 = (zq1, zv1, zo1, zg1)

    @pl.when(t == 0)
    def _():
        h0[...] = jnp.zeros(h0.shape, BF16)
        zq1[...] = jnp.zeros(zq1.shape, F32)
        zv1[...] = jnp.zeros(zv1.shape, BF16)
        zo1[...] = jnp.zeros(zo1.shape, F32)
        zg1[...] = jnp.zeros(zg1.shape, F32)

    step = functools.partial(
        _odd_step, t, n_seq_tiles, xc_ref, xo_ref, pre_g, w_main, w_gate, gate_bias, conv_w,
        h_gain, w_out, post_g, o_ref)

    @pl.when(t % 2 == 0)
    def _():
        step(slot0, slot1, h1, h0, c_st, n_st, m_st, y_buf)

    @pl.when(t % 2 == 1)
    def _():
        step(slot1, slot0, h0, h1, c_st, n_st, m_st, y_buf)


def _ffn_kernel(x_ref, p_ref, pre_g, w_gu, w_down, post_g, w_pgate, w_pproj, ple_g, o_ref):
    x = x_ref[...]
    hn = _rms(x, pre_g[...]).astype(BF16)
    def gate_up(start, size):
        g = jnp.dot(hn, w_gu[:, start:start + size], preferred_element_type=F32)
        u = jnp.dot(hn, w_gu[:, D_FF + start:D_FF + start + size], preferred_element_type=F32)
        return g, u

    acc = None
    pending = gate_up(*FF_CHUNKS[0])
    for j, (start, size) in enumerate(FF_CHUNKS):
        g, u = pending
        if j + 1 < len(FF_CHUNKS):
            pending = gate_up(*FF_CHUNKS[j + 1])
        act = (_silu(g) * u).astype(BF16)
        d = jnp.dot(act, w_down[start:start + size, :], preferred_element_type=F32)
        acc = d if acc is None else acc + d
    emb = _dot(p_ref[...], w_pproj[...])
    x2 = x + _rms(acc, post_g[...])
    e = _sigmoid(_dot(x2, w_pgate[...])) * emb
    o_ref[...] = x2 + _rms(e, ple_g[...])


def _layer_spec(stacked, layer):
    rest = stacked.shape[1:]
    zeros = (0,) * len(rest)
    return pl.BlockSpec((None,) + rest, lambda *_: (layer,) + zeros, pipeline_mode=pl.Buffered(1))


def _row_spec(n_seq_tiles, width, tm=TM):
    return pl.BlockSpec((tm, width), lambda b, s: (b * n_seq_tiles + s, 0))


_PARAMS = pltpu.CompilerParams(dimension_semantics=("arbitrary", "arbitrary"),
                               vmem_limit_bytes=VMEM_LIMIT)


def _even_call(x2d, batch, seq, layers, stacked):
    ns = seq // TM_EVEN
    return pl.pallas_call(
        _even_kernel,
        out_shape=jax.ShapeDtypeStruct(x2d.shape, F32),
        grid=(batch, ns),
        in_specs=[_row_spec(ns, D_MODEL, TM_EVEN)]
                 + [_layer_spec(a, l) for a, l in zip(stacked, layers)],
        out_specs=_row_spec(ns, D_MODEL, TM_EVEN),
        scratch_shapes=[pltpu.VMEM((POOL_HALO + TM_EVEN, B_WIDTH), F32),
                        pltpu.VMEM((POOL_HALO + TM_EVEN, B_GROUP_DIM), F32),
                        pltpu.VMEM((POOL_HALO + TM_EVEN, B_GROUP_DIM), F32),
                        pltpu.VMEM((TM_EVEN, EVEN_OUT), BF16)],
        compiler_params=_PARAMS,
        name="even_mixer",
    )(x2d, *stacked)


def _odd_call(x2d, batch, seq, layers, stacked):
    ns = seq // TM
    n_tiles = batch * ns
    cur_spec = pl.BlockSpec((TM, D_MODEL), lambda t: (jnp.minimum(t, n_tiles - 1), 0))
    old_spec = pl.BlockSpec((TM, D_MODEL), lambda t: (jnp.maximum(t - 2, 0), 0))
    slot = lambda width, dtype, halo=0: [pltpu.VMEM((halo + TM, width), dtype)] * 2
    return pl.pallas_call(
        functools.partial(_odd_kernel, n_seq_tiles=ns),
        out_shape=jax.ShapeDtypeStruct(x2d.shape, F32),
        grid=(n_tiles + 2,),
        in_specs=[cur_spec, old_spec] + [_layer_spec(a, l) for a, l in zip(stacked, layers)],
        out_specs=old_spec,
        scratch_shapes=(slot(2 * C_QK_WIDTH, F32, CONV_HALO) + slot(C_V_WIDTH, BF16)
                        + slot(C_V_WIDTH, F32) + slot(2 * LANES, F32) + slot(C_V_WIDTH, BF16)
                        + [pltpu.VMEM((C_HEADS, C_QK_DIM, C_V_DIM), F32),
                           pltpu.VMEM((C_HEADS, 1, C_QK_DIM), F32),
                           pltpu.VMEM((C_HEADS, 1, LANES), F32),
                           pltpu.VMEM((TM, D_MODEL), F32)]),
        compiler_params=pltpu.CompilerParams(dimension_semantics=("arbitrary",),
                                             vmem_limit_bytes=VMEM_LIMIT),
        name="odd_mixer",
    )(x2d, x2d, *stacked)


def _ffn_call(x2d, p3d, batch, seq, layer, stacked):
    ns = seq // TM
    p_spec = pl.BlockSpec((None, TM, PLE_DIM), lambda b, s: (layer, b * ns + s, 0))
    return pl.pallas_call(
        _ffn_kernel,
        out_shape=jax.ShapeDtypeStruct(x2d.shape, F32),
        grid=(batch, ns),
        in_specs=[_row_spec(ns, D_MODEL), p_spec] + [_layer_spec(a, layer) for a in stacked],
        out_specs=_row_spec(ns, D_MODEL),
        compiler_params=_PARAMS,
        name="ffn_ple",
    )(x2d, p3d, *stacked)


def kernel(x, p, mix_pre_gain, mix_post_gain, ffn_pre_gain, ffn_post_gain, ple_post_gain, even_w_in, even_a_v_gain, even_a_ws, even_a_bs, even_b_wpool, even_b_scale, even_w_out, odd_w_in, odd_conv_w, odd_b_i, odd_b_f, odd_h_gain, odd_w_out, ffn_w_gate_up, ffn_w_down, ple_proj, ple_gate):
    batch, seq, d = x.shape
    depth = p.shape[0]
    assert d == D_MODEL and seq % TM == 0 and TM % C_CHUNK == 0
    assert seq % TM_EVEN == 0 and TM_EVEN % A_CHUNK == 0
    rows = lambda a: a[:, None, :]

    mix_pre, mix_post = rows(mix_pre_gain), rows(mix_post_gain)
    even_stacked = (mix_pre, even_w_in.astype(BF16), rows(even_a_v_gain), even_a_ws,
                    jnp.swapaxes(even_a_bs, 1, 2), even_b_wpool.astype(BF16), rows(even_b_scale),
                    even_w_out.astype(BF16), mix_post)
    n_odd = odd_w_in.shape[0]
    w_gate = jnp.zeros((n_odd, d, 2 * LANES), F32)
    w_gate = w_gate.at[:, :, :C_HEADS].set(odd_w_in[:, :, ODD_MAIN:ODD_MAIN + C_HEADS])
    w_gate = w_gate.at[:, :, LANES:LANES + C_HEADS].set(odd_w_in[:, :, ODD_MAIN + C_HEADS:])
    gate_bias = jnp.zeros((n_odd, 1, 2 * LANES), F32)
    gate_bias = gate_bias.at[:, 0, :C_HEADS].set(odd_b_i)
    gate_bias = gate_bias.at[:, 0, LANES:LANES + C_HEADS].set(odd_b_f)
    odd_stacked = (mix_pre, odd_w_in.astype(BF16), w_gate.astype(BF16), gate_bias,
                   odd_conv_w, rows(odd_h_gain), odd_w_out.astype(BF16), mix_post)
    ffn_stacked = (rows(ffn_pre_gain), ffn_w_gate_up.astype(BF16), ffn_w_down.astype(BF16),
                   rows(ffn_post_gain), ple_gate.astype(BF16), ple_proj.astype(BF16),
                   rows(ple_post_gain))

    x2d = x.reshape(batch * seq, d)
    p3d = p.reshape(depth, batch * seq, PLE_DIM)
    for i in range(depth):
        j = i // 2
        if i % 2 == 0:
            x2d = _even_call(x2d, batch, seq, (i, j, j, j, j, j, j, j, i), even_stacked)
        else:
            x2d = _odd_call(x2d, batch, seq, (i, j, j, j, j, j, j, i), odd_stacked)
        x2d = _ffn_call(x2d, p3d, batch, seq, i, ffn_stacked)
    return x2d.reshape(batch, seq, d)
```

```python
import functools

import jax
import jax.numpy as jnp
from jax import lax
from jax.experimental import pallas as pl
from jax.experimental.pallas import tpu as pltpu

D_MODEL = 1024
A_HEADS = 4
A_HEAD_DIM = 128
A_WIDTH = A_HEADS * A_HEAD_DIM
A_CHUNK = 128
B_GROUPS = 4
B_GROUP_DIM = 128
B_WIDTH = B_GROUPS * B_GROUP_DIM
B_WINDOWS = (2, 4, 8, 16)
EVEN_IN = 2 * A_WIDTH + B_WIDTH
EVEN_OUT = A_WIDTH + B_WIDTH
C_HEADS = 4
C_QK_DIM = 128
C_V_DIM = 256
C_QK_WIDTH = C_HEADS * C_QK_DIM
C_V_WIDTH = C_HEADS * C_V_DIM
C_CHUNK = 128
C_CONV = 4
ODD_MAIN = 2 * C_QK_WIDTH + 2 * C_V_WIDTH
D_FF = 2816
PLE_DIM = 256
EPS = 1e-6

LANES = 128
SUBLANES = 8
POOL_HALO = 32
CONV_HALO = 8
TM = 512
TM_EVEN = 1024
EVEN_SPLIT = 4
FFN_SPLIT = 2
FF_CHUNKS = ((0, 1024), (1024, 1024), (2048, 768))
PIECE = 256
VMEM_LIMIT = 56 * 1024 * 1024

F32 = jnp.float32
BF16 = jnp.bfloat16


def _rms(x, g):
    ms = jnp.mean(x * x, axis=-1, keepdims=True)
    return x * lax.rsqrt(ms + EPS) * g


def _dot(a, b):
    return jnp.dot(a.astype(BF16), b.astype(BF16), preferred_element_type=F32)


def _sigmoid(x):
    return 1.0 / (1.0 + jnp.exp(-x))


def _silu(x):
    return x * _sigmoid(x)


def _gelu_tanh(x):
    return 0.5 * x * (1.0 + jnp.tanh(0.7978845608028654 * (x + 0.044715 * (x * x * x))))


def _log_sigmoid(x):
    return jnp.minimum(x, 0.0) - jnp.log(1.0 + jnp.exp(-jnp.abs(x)))


def _split3(x):
    hi = x.astype(BF16)
    r = x - hi.astype(F32)
    mid = r.astype(BF16)
    lo = (r - mid.astype(F32)).astype(BF16)
    return hi, mid, lo


def _even_kernel(x_ref, pre_g, w_in, v_gain, ws, bs_t, w_pool, pool_scale, w_out, post_g,
                 o_ref, xb_buf, tmp_a, tmp_b, mix_buf):
    si = pl.program_id(1)
    tm = x_ref.shape[0]
    H = POOL_HALO
    parts = [slice(k * (tm // EVEN_SPLIT), (k + 1) * (tm // EVEN_SPLIT)) for k in range(EVEN_SPLIT)]

    @pl.when(si == 0)
    def _():
        xb_buf[0:H, :] = jnp.zeros((H, B_WIDTH), F32)

    @pl.when(si > 0)
    def _():
        xb_buf[0:H, :] = xb_buf[tm:tm + H, :]

    zv, zu = [], []
    for rs in parts:
        h = _rms(x_ref[rs, :], pre_g[...]).astype(BF16)
        xb_buf[H + rs.start:H + rs.stop, :] = jnp.dot(h, w_in[:, 2 * A_WIDTH:],
                                                     preferred_element_type=F32)
        zv.append(jnp.dot(h, w_in[:, A_WIDTH:2 * A_WIDTH], preferred_element_type=F32))
        zu.append(jnp.dot(h, w_in[:, :A_WIDTH], preferred_element_type=F32))
    zv = jnp.concatenate(zv, axis=0)
    zu = jnp.concatenate(zu, axis=0)

    t_pos = (si * tm + lax.broadcasted_iota(jnp.int32, (tm, 1), 0)).astype(F32)
    for g, win in enumerate(B_WINDOWS):
        cs = slice(g * B_GROUP_DIM, (g + 1) * B_GROUP_DIM)
        levels = win.bit_length() - 1
        src, src_cols, dst = xb_buf, cs, tmp_a
        acc = None
        for j in range(1, levels + 1):
            step = 1 << (j - 1)
            lo = H - SUBLANES * (levels - j)
            if step % SUBLANES == 0:
                acc = acc[step:, :] + acc[:-step, :]
            else:
                if j > 1:
                    src[lo - SUBLANES:H + tm, src_cols] = acc
                acc = src[lo:H + tm, src_cols] + src[lo - step:H + tm - step, src_cols]
            src, src_cols, dst = dst, slice(None), (tmp_b if dst is tmp_a else tmp_a)
        cur = xb_buf[H:H + tm, cs]
        cnt = jnp.minimum(t_pos + 1.0, float(win))
        pooled = acc / cnt - cur
        yb = _dot(pooled, w_pool[g]) * pool_scale[:, cs]
        mix_buf[:, A_WIDTH + g * B_GROUP_DIM:A_WIDTH + (g + 1) * B_GROUP_DIM] = yb.astype(BF16)

    v = _rms(_gelu_tanh(zv), v_gain[...]).astype(BF16)
    u = _gelu_tanh(zu)
    row = lax.broadcasted_iota(jnp.int32, (A_CHUNK, A_CHUNK), 0)
    col = lax.broadcasted_iota(jnp.int32, (A_CHUNK, A_CHUNK), 1)
    causal = col <= row
    for hd in range(A_HEADS):
        w_h = jnp.where(causal, ws[hd], 0.0).astype(BF16)
        bias = bs_t[:, hd:hd + 1]
        cs = slice(hd * A_HEAD_DIM, (hd + 1) * A_HEAD_DIM)
        for c in range(tm // A_CHUNK):
            rs = slice(c * A_CHUNK, (c + 1) * A_CHUNK)
            sv = jnp.dot(w_h, v[rs, cs], preferred_element_type=F32) + bias
            mix_buf[rs, cs] = (u[rs, cs] * sv).astype(BF16)

    for rs in parts:
        y = jnp.dot(mix_buf[rs, :], w_out[...], preferred_element_type=F32)
        o_ref[rs, :] = x_ref[rs, :] + _rms(y, post_g[...])


def _odd_step(t, n_seq_tiles, xc_ref, xo_ref, pre_g, w_main, w_gate, gate_bias, conv_w, h_gain,
              w_out, post_g, o_ref, cur, prev, h_new, h_old, c_st, n_st, m_st, y_buf):
    zq_c, zv_c, zo_c, zg_c = cur
    zq_p, zv_p, zo_p, zg_p = prev
    tm = xc_ref.shape[0]
    L = C_CHUNK
    seq_start = (jnp.maximum(t - 1, 0) % n_seq_tiles) == 0

    @pl.when(seq_start)
    def _():
        zq_p[0:CONV_HALO, :] = jnp.zeros((CONV_HALO, 2 * C_QK_WIDTH), F32)
        c_st[...] = jnp.zeros(c_st.shape, F32)
        n_st[...] = jnp.zeros(n_st.shape, F32)
        m_st[...] = jnp.zeros(m_st.shape, F32)

    @pl.when(jnp.logical_not(seq_start))
    def _():
        zq_p[0:CONV_HALO, :] = zq_c[tm:tm + CONV_HALO, :]

    hb = _rms(xc_ref[...], pre_g[...]).astype(BF16)

    def project_piece(j):
        cols = slice(j * PIECE, (j + 1) * PIECE)
        z = jnp.dot(hb, w_main[:, cols], preferred_element_type=F32)
        if cols.stop <= 2 * C_QK_WIDTH:
            zq_c[CONV_HALO:CONV_HALO + tm, cols] = z
        elif cols.stop <= 2 * C_QK_WIDTH + C_V_WIDTH:
            zv_c[:, cols.start - 2 * C_QK_WIDTH:cols.stop - 2 * C_QK_WIDTH] = z.astype(BF16)
        else:
            off = 2 * C_QK_WIDTH + C_V_WIDTH
            zo_c[:, cols.start - off:cols.stop - off] = z

    def project_gates():
        zg_c[...] = jnp.dot(hb, w_gate[...], preferred_element_type=F32) + gate_bias[...]

    def out_piece(j):
        cols = slice(j * PIECE, (j + 1) * PIECE)
        y_buf[:, cols] = jnp.dot(h_old[...], w_out[:, cols], preferred_element_type=F32)

    qk_pieces = 2 * C_QK_WIDTH // PIECE
    v_pieces = C_V_WIDTH // PIECE
    fill_head = [functools.partial(project_piece, j) for j in range(qk_pieces)]
    fill_intra = ([functools.partial(project_piece, qk_pieces + j) for j in range(v_pieces)]
                  + [project_gates])
    fill_rec = ([functools.partial(project_piece, qk_pieces + v_pieces + j) for j in range(v_pieces)]
                + [functools.partial(out_piece, j) for j in range(D_MODEL // PIECE)])

    def conv_silu(lo_col):
        cols = slice(lo_col, lo_col + C_QK_DIM)
        conv = zq_p[CONV_HALO:CONV_HALO + tm, cols] * conv_w[0:1, cols]
        for k in range(1, C_CONV):
            conv = conv + zq_p[CONV_HALO - k:CONV_HALO - k + tm, cols] * conv_w[k:k + 1, cols]
        return _silu(conv)

    row = lax.broadcasted_iota(jnp.int32, (L, L), 0)
    col = lax.broadcasted_iota(jnp.int32, (L, L), 1)
    causal = col <= row
    tri = jnp.where(causal, 1.0, 0.0).astype(BF16)

    n_chunks = tm // L
    rows_of = lambda c: slice(c * L, (c + 1) * L)
    v_of = lambda hd: slice(hd * C_V_DIM, (hd + 1) * C_V_DIM)
    pairs = [(c, hd) for c in range(n_chunks) for hd in range(C_HEADS)]

    ig_all = zg_p[:, :LANES]
    hi, mid, lo = _split3(_log_sigmoid(zg_p[:, LANES:]))
    gates = []
    for c in range(n_chunks):
        rs = rows_of(c)
        b_c = (jnp.dot(tri, hi[rs], preferred_element_type=F32)
               + jnp.dot(tri, mid[rs], preferred_element_type=F32)
               + jnp.dot(tri, lo[rs], preferred_element_type=F32))
        gates.append((ig_all[rs], b_c, ig_all[rs].T, b_c.T))
    q_of, k_of, qb_of, kb_of, qk_t = [], [], [], [], {}
    for hd in range(C_HEADS):
        q_of.append(conv_silu(hd * C_QK_DIM) * (C_QK_DIM ** -0.5))
        k_of.append(conv_silu(C_QK_WIDTH + hd * C_QK_DIM))
        qb_of.append(q_of[hd].astype(BF16))
        kb_of.append(k_of[hd].astype(BF16))
        if hd < len(fill_head):
            fill_head[hd]()
        for c in range(n_chunks):
            qk_t[c, hd] = lax.dot_general(qb_of[hd][rows_of(c)], kb_of[hd][rows_of(c)],
                                          (((1,), (1,)), ((), ())), preferred_element_type=F32)
    intra = {}
    for i, (c, hd) in enumerate(pairs):
        if i % 3 == 0 and i // 3 < len(fill_intra):
            fill_intra[i // 3]()
        rs = rows_of(c)
        ig_c, b_c, ig_t, b_t = gates[c]
        b_col = b_c[:, hd:hd + 1]
        ig_col = ig_c[:, hd:hd + 1]
        vh = zv_p[rs, v_of(hd)]
        dlog = jnp.where(causal, b_col - b_t[hd:hd + 1, :] + ig_t[hd:hd + 1, :], -jnp.inf)
        mt_i = jnp.max(dlog, axis=-1, keepdims=True)
        sc = qk_t[c, hd] * jnp.exp(dlog - mt_i)
        sv = jnp.dot(sc.astype(BF16), vh, preferred_element_type=F32)
        s_sum = jnp.sum(sc, axis=-1, keepdims=True)
        b_last = b_c[L - 1:L, hd:hd + 1]
        g_col = b_last - b_col + ig_col
        mg = jnp.max(g_col, axis=0, keepdims=True)
        ks = k_of[hd][rs] * jnp.exp(g_col - mg)
        kv = _dot(ks.T, vh)
        k_sum = jnp.sum(ks, axis=0, keepdims=True)
        intra[c, hd] = (b_col, b_last, mt_i, sv, s_sum, mg, kv, k_sum)

    state = [(c_st[hd], n_st[hd], m_st[hd][:, 0:1]) for hd in range(C_HEADS)]
    for i, (c, hd) in enumerate(pairs):
        if i % 2 == 0:
            fill_rec[i // 2]()
        rs = rows_of(c)
        c_prev, n_prev, m_prev = state[hd]
        b_col, b_last, mt_i, sv, s_sum, mg, kv, k_sum = intra[c, hd]
        a = b_col + m_prev
        mt = jnp.maximum(a, mt_i)
        wi = jnp.exp(a - mt)
        wn = jnp.exp(mt_i - mt)
        num = wi * _dot(qb_of[hd][rs], c_prev) + wn * sv
        den = wi * jnp.sum(q_of[hd][rs] * n_prev, axis=-1, keepdims=True) + wn * s_sum
        hh = num / jnp.maximum(jnp.abs(den), jnp.exp(-mt))
        m_new = jnp.maximum(b_last + m_prev, mg)
        wc = jnp.exp(b_last + m_prev - m_new)
        wk = jnp.exp(mg - m_new)
        state[hd] = (wc * c_prev + wk * kv, wc * n_prev + wk * k_sum, m_new)
        hn = _rms(hh, h_gain[:, v_of(hd)])
        h_new[rs, v_of(hd)] = (hn * _sigmoid(zo_p[rs, v_of(hd)])).astype(BF16)
    for hd in range(C_HEADS):
        c_st[hd], n_st[hd] = state[hd][0], state[hd][1]
        m_st[hd] = jnp.broadcast_to(state[hd][2], (1, LANES))

    o_ref[...] = xo_ref[...] + _rms(y_buf[...], post_g[...])


def _odd_kernel(xc_ref, xo_ref, pre_g, w_main, w_gate, gate_bias, conv_w, h_gain, w_out, post_g,
                o_ref, zq0, zq1, zv0, zv1, zo0, zo1, zg0, zg1, h0, h1, c_st, n_st, m_st, y_buf,
                *, n_seq_tiles):
    t = pl.program_id(0)
    slot0 = (zq0, zv0, zo0, zg0)
    slot1 = (zq1, zv1, zo1, zg1)

    @pl.when(t == 0)
    def _():
        h0[...] = jnp.zeros(h0.shape, BF16)
        zq1[...] = jnp.zeros(zq1.shape, F32)
        zv1[...] = jnp.zeros(zv1.shape, BF16)
        zo1[...] = jnp.zeros(zo1.shape, F32)
        zg1[...] = jnp.zeros(zg1.shape, F32)

    step = functools.partial(
        _odd_step, t, n_seq_tiles, xc_ref, xo_ref, pre_g, w_main, w_gate, gate_bias, conv_w,
        h_gain, w_out, post_g, o_ref)

    @pl.when(t % 2 == 0)
    def _():
        step(slot0, slot1, h1, h0, c_st, n_st, m_st, y_buf)

    @pl.when(t % 2 == 1)
    def _():
        step(slot1, slot0, h0, h1, c_st, n_st, m_st, y_buf)


def _ffn_kernel(x_ref, p_ref, pre_g, w_gu, w_down, post_g, w_pgate, w_pproj, ple_g, o_ref):
    x = x_ref[...]
    hn = _rms(x, pre_g[...]).astype(BF16)
    def gate_up(start, size):
        g = jnp.dot(hn, w_gu[:, start:start + size], preferred_element_type=F32)
        u = jnp.dot(hn, w_gu[:, D_FF + start:D_FF + start + size], preferred_element_type=F32)
        return g, u

    acc = None
    pending = gate_up(*FF_CHUNKS[0])
    for j, (start, size) in enumerate(FF_CHUNKS):
        g, u = pending
        if j + 1 < len(FF_CHUNKS):
            pending = gate_up(*FF_CHUNKS[j + 1])
        act = (_silu(g) * u).astype(BF16)
        d = jnp.dot(act, w_down[start:start + size, :], preferred_element_type=F32)
        acc = d if acc is None else acc + d
    emb = _dot(p_ref[...], w_pproj[...])
    n_rows = x.shape[0]
    for k in range(FFN_SPLIT):
        rs = slice(k * (n_rows // FFN_SPLIT), (k + 1) * (n_rows // FFN_SPLIT))
        x2 = x[rs] + _rms(acc[rs], post_g[...])
        e = _sigmoid(_dot(x2, w_pgate[...])) * emb[rs]
        o_ref[rs, :] = x2 + _rms(e, ple_g[...])


def _layer_spec(stacked, layer):
    rest = stacked.shape[1:]
    zeros = (0,) * len(rest)
    return pl.BlockSpec((None,) + rest, lambda *_: (layer,) + zeros, pipeline_mode=pl.Buffered(1))


def _row_spec(n_seq_tiles, width, tm=TM):
    return pl.BlockSpec((tm, width), lambda b, s: (b * n_seq_tiles + s, 0))


_PARAMS = pltpu.CompilerParams(dimension_semantics=("arbitrary", "arbitrary"),
                               vmem_limit_bytes=VMEM_LIMIT)


def _even_call(x2d, batch, seq, layers, stacked):
    ns = seq // TM_EVEN
    return pl.pallas_call(
        _even_kernel,
        out_shape=jax.ShapeDtypeStruct(x2d.shape, F32),
        grid=(batch, ns),
        in_specs=[_row_spec(ns, D_MODEL, TM_EVEN)]
                 + [_layer_spec(a, l) for a, l in zip(stacked, layers)],
        out_specs=_row_spec(ns, D_MODEL, TM_EVEN),
        scratch_shapes=[pltpu.VMEM((POOL_HALO + TM_EVEN, B_WIDTH), F32),
                        pltpu.VMEM((POOL_HALO + TM_EVEN, B_GROUP_DIM), F32),
                        pltpu.VMEM((POOL_HALO + TM_EVEN, B_GROUP_DIM), F32),
                        pltpu.VMEM((TM_EVEN, EVEN_OUT), BF16)],
        compiler_params=_PARAMS,
        name="even_mixer",
    )(x2d, *stacked)


def _odd_call(x2d, batch, seq, layers, stacked):
    ns = seq // TM
    n_tiles = batch * ns
    cur_spec = pl.BlockSpec((TM, D_MODEL), lambda t: (jnp.minimum(t, n_tiles - 1), 0))
    old_spec = pl.BlockSpec((TM, D_MODEL), lambda t: (jnp.maximum(t - 2, 0), 0))
    slot = lambda width, dtype, halo=0: [pltpu.VMEM((halo + TM, width), dtype)] * 2
    return pl.pallas_call(
        functools.partial(_odd_kernel, n_seq_tiles=ns),
        out_shape=jax.ShapeDtypeStruct(x2d.shape, F32),
        grid=(n_tiles + 2,),
        in_specs=[cur_spec, old_spec] + [_layer_spec(a, l) for a, l in zip(stacked, layers)],
        out_specs=old_spec,
        scratch_shapes=(slot(2 * C_QK_WIDTH, F32, CONV_HALO) + slot(C_V_WIDTH, BF16)
                        + slot(C_V_WIDTH, F32) + slot(2 * LANES, F32) + slot(C_V_WIDTH, BF16)
                        + [pltpu.VMEM((C_HEADS, C_QK_DIM, C_V_DIM), F32),
                           pltpu.VMEM((C_HEADS, 1, C_QK_DIM), F32),
                           pltpu.VMEM((C_HEADS, 1, LANES), F32),
                           pltpu.VMEM((TM, D_MODEL), F32)]),
        compiler_params=pltpu.CompilerParams(dimension_semantics=("arbitrary",),
                                             vmem_limit_bytes=VMEM_LIMIT),
        name="odd_mixer",
    )(x2d, x2d, *stacked)


def _ffn_call(x2d, p3d, batch, seq, layer, stacked):
    ns = seq // TM
    p_spec = pl.BlockSpec((None, TM, PLE_DIM), lambda b, s: (layer, b * ns + s, 0))
    return pl.pallas_call(
        _ffn_kernel,
        out_shape=jax.ShapeDtypeStruct(x2d.shape, F32),
        grid=(batch, ns),
        in_specs=[_row_spec(ns, D_MODEL), p_spec] + [_layer_spec(a, layer) for a in stacked],
        out_specs=_row_spec(ns, D_MODEL),
        compiler_params=_PARAMS,
        name="ffn_ple",
    )(x2d, p3d, *stacked)


def kernel(x, p, mix_pre_gain, mix_post_gain, ffn_pre_gain, ffn_post_gain, ple_post_gain, even_w_in, even_a_v_gain, even_a_ws, even_a_bs, even_b_wpool, even_b_scale, even_w_out, odd_w_in, odd_conv_w, odd_b_i, odd_b_f, odd_h_gain, odd_w_out, ffn_w_gate_up, ffn_w_down, ple_proj, ple_gate):
    batch, seq, d = x.shape
    depth = p.shape[0]
    assert d == D_MODEL and seq % TM == 0 and TM % C_CHUNK == 0
    assert seq % TM_EVEN == 0 and TM_EVEN % A_CHUNK == 0
    rows = lambda a: a[:, None, :]

    mix_pre, mix_post = rows(mix_pre_gain), rows(mix_post_gain)
    even_stacked = (mix_pre, even_w_in.astype(BF16), rows(even_a_v_gain), even_a_ws,
                    jnp.swapaxes(even_a_bs, 1, 2), even_b_wpool.astype(BF16), rows(even_b_scale),
                    even_w_out.astype(BF16), mix_post)
    n_odd = odd_w_in.shape[0]
    w_gate = jnp.zeros((n_odd, d, 2 * LANES), F32)
    w_gate = w_gate.at[:, :, :C_HEADS].set(odd_w_in[:, :, ODD_MAIN:ODD_MAIN + C_HEADS])
    w_gate = w_gate.at[:, :, LANES:LANES + C_HEADS].set(odd_w_in[:, :, ODD_MAIN + C_HEADS:])
    gate_bias = jnp.zeros((n_odd, 1, 2 * LANES), F32)
    gate_bias = gate_bias.at[:, 0, :C_HEADS].set(odd_b_i)
    gate_bias = gate_bias.at[:, 0, LANES:LANES + C_HEADS].set(odd_b_f)
    odd_stacked = (mix_pre, odd_w_in.astype(BF16), w_gate.astype(BF16), gate_bias,
                   odd_conv_w, rows(odd_h_gain), odd_w_out.astype(BF16), mix_post)
    ffn_stacked = (rows(ffn_pre_gain), ffn_w_gate_up.astype(BF16), ffn_w_down.astype(BF16),
                   rows(ffn_post_gain), ple_gate.astype(BF16), ple_proj.astype(BF16),
                   rows(ple_post_gain))

    x2d = x.reshape(batch * seq, d)
    p3d = p.reshape(depth, batch * seq, PLE_DIM)
    for i in range(depth):
        j = i // 2
        if i % 2 == 0:
            x2d = _even_call(x2d, batch, seq, (i, j, j, j, j, j, j, j, i), even_stacked)
        else:
            x2d = _odd_call(x2d, batch, seq, (i, j, j, j, j, j, j, i), odd_stacked)
        x2d = _ffn_call(x2d, p3d, batch, seq, i, ffn_stacked)
    return x2d.reshape(batch, seq, d)
```
